```python
import math
import jax, jax.numpy as jnp
from jax import lax
import numpy as np

D_MODEL = 1024
BATCH = 8
SEQ = 2048
DEPTH = 4
DEC_BATCH = 4
DEC_SEQ = 4096
PAST_LEN = 128

HEAD_DIM = 64
QB = 128
WINDOW = 128
A_HEADS = 8
A_KV_HEADS = 2
B_HEADS = 8
B_Q_LORA = 384
B_KV_LORA = 256
B_NOPE = 64
B_ROPE = 32
B_V = 64
C_HEADS = 8
C_KV_HEADS = 2
D_HEADS = 4
X_HEADS = 4
X_HEAD_DIM = D_MODEL // X_HEADS
N_MEM = 256
N_BUCKETS = 32
MAX_DISTANCE = 128
N_EXPERTS = 16
EXPERT_FF = 2 * D_MODEL
EC_CAPACITY = 2
GRID_W = 64
ROPE_BASE = 10000.0
N_EVEN = (DEPTH + 1) // 2
N_ODD = DEPTH // 2
EVEN_SIZES = (A_HEADS * HEAD_DIM, A_KV_HEADS * HEAD_DIM, A_KV_HEADS * HEAD_DIM, B_Q_LORA, B_KV_LORA, B_ROPE)
ODD_SIZES = (C_HEADS * HEAD_DIM, C_KV_HEADS * HEAD_DIM, C_KV_HEADS * HEAD_DIM,
             D_HEADS * 2 * HEAD_DIM, D_HEADS * 2 * HEAD_DIM, D_HEADS * 2 * HEAD_DIM)
EVEN_IN = sum(EVEN_SIZES)
ODD_IN = sum(ODD_SIZES)
MIX_OUT = A_HEADS * HEAD_DIM + B_HEADS * B_V
NEG_INF = -1e30

kernel_name = 'hybrid_bidir_encoder_swa_mla_axial_diff_ec'


def rms_norm(x, g, eps=1e-6):
    xf = x.astype(jnp.float32)
    y = xf * lax.rsqrt(jnp.mean(xf * xf, axis=-1, keepdims=True) + eps)
    return (y * g.astype(jnp.float32)).astype(x.dtype)


def split_cols(z, sizes):
    out, off = [], 0
    for s in sizes:
        out.append(z[..., off:off + s])
        off += s
    return out


def rope_table(pos, dim):
    freqs = ROPE_BASE ** (-jnp.arange(0, dim, 2, dtype=jnp.float32) / dim)
    ang = pos.astype(jnp.float32)[:, None] * freqs[None, :]
    return jnp.cos(ang), jnp.sin(ang)


def apply_rope(x, cos, sin):
    x1, x2 = jnp.split(x, 2, axis=-1)
    c = cos[None, :, None, :].astype(x.dtype)
    s = sin[None, :, None, :].astype(x.dtype)
    return jnp.concatenate([x1 * c - x2 * s, x1 * s + x2 * c], axis=-1)


def t5_bucket(rel):
    half = N_BUCKETS // 2
    exact = half // 2
    n = jnp.abs(rel)
    big = exact + (jnp.log(jnp.maximum(n, 1).astype(jnp.float32) / exact)
                   / math.log(MAX_DISTANCE / exact) * (half - exact)).astype(jnp.int32)
    big = jnp.minimum(big, half - 1)
    return jnp.where(rel > 0, half, 0) + jnp.where(n < exact, n, big)


def sweep_query_blocks(fn, qs):
    S = qs[0].shape[1]
    nb = S // QB
    def split(a):
        return jnp.moveaxis(a.reshape(a.shape[0], nb, QB, *a.shape[2:]), 1, 0)
    starts = jnp.arange(nb, dtype=jnp.int32) * QB
    out = lax.map(lambda xs: fn(xs[0], xs[1]), (tuple(split(a) for a in qs), starts))
    out = jnp.moveaxis(out, 0, 1)
    return out.reshape(out.shape[0], S, *out.shape[3:])


def window_gqa_sink(q, k, v, sink, bias):
    B, S, H, d = q.shape
    nb = S // QB
    G = H // A_KV_HEADS
    qb = q.reshape(B, nb, QB, A_KV_HEADS, G, d)
    pad = ((0, 0), (QB, QB), (0, 0), (0, 0))
    kp = jnp.pad(k, pad).reshape(B, nb + 2, QB, A_KV_HEADS, d)
    vp = jnp.pad(v, pad).reshape(B, nb + 2, QB, A_KV_HEADS, d)
    kw = jnp.concatenate([kp[:, :-2], kp[:, 1:-1], kp[:, 2:]], axis=2)
    vw = jnp.concatenate([vp[:, :-2], vp[:, 1:-1], vp[:, 2:]], axis=2)
    s = jnp.einsum('bnqhgd,bnkhd->bnhgqk', qb, kw).astype(jnp.float32) * (d ** -0.5)
    s = s + bias.reshape(A_KV_HEADS, G, QB, 3 * QB)
    qi = jnp.arange(QB)
    kj = jnp.arange(3 * QB)
    rel = kj[None, :] - QB - qi[:, None]
    kpos = jnp.arange(nb)[:, None] * QB - QB + kj[None, :]
    mask = (jnp.abs(rel) <= WINDOW)[None] & ((kpos >= 0) & (kpos < S))[:, None, :]
    s = jnp.where(mask[None, :, None, None], s, NEG_INF)
    sk = sink.astype(jnp.float32).reshape(A_KV_HEADS, G)[None, None, :, :, None, None]
    m = jnp.maximum(jnp.max(s, axis=-1, keepdims=True), sk)
    p = jnp.exp(s - m)
    p = p / (jnp.sum(p, axis=-1, keepdims=True) + jnp.exp(sk - m))
    o = jnp.einsum('bnhgqk,bnkhd->bnqhgd', p.astype(v.dtype), vw)
    return o.reshape(B, S, H * d)


def mla(c_q, c_kv, k_r, g_q, g_kv, w_uq, w_ukv, cos_t, sin_t):
    B, S, _ = c_q.shape
    q = (rms_norm(c_q, g_q) @ w_uq).reshape(B, S, B_HEADS, B_NOPE + B_ROPE)
    q_nope = q[..., :B_NOPE]
    q_rope = apply_rope(q[..., B_NOPE:], cos_t, sin_t)
    kv = (rms_norm(c_kv, g_kv) @ w_ukv).reshape(B, S, B_HEADS, B_NOPE + B_V)
    k_nope, v = kv[..., :B_NOPE], kv[..., B_NOPE:]
    k_rope = apply_rope(k_r[:, :, None, :], cos_t, sin_t)[:, :, 0]
    scale = (B_NOPE + B_ROPE) ** -0.5
    def block(qs, start):
        qn, qr = qs
        s = (jnp.einsum('bqhd,bkhd->bhqk', qn, k_nope)
             + jnp.einsum('bqhd,bkd->bhqk', qr, k_rope)).astype(jnp.float32) * scale
        p = jax.nn.softmax(s, axis=-1).astype(v.dtype)
        return jnp.einsum('bhqk,bkhd->bqhd', p, v)
    o = sweep_query_blocks(block, (q_nope, q_rope))
    return o.reshape(B, S, B_HEADS * B_V)


def axial_gqa(q, k, v, g_q, g_k, cos_r, sin_r, cos_c, sin_c):
    B, S, H, d = q.shape
    half = d // 2
    def axial(x):
        return jnp.concatenate([apply_rope(x[..., :half], cos_r, sin_r),
                                apply_rope(x[..., half:], cos_c, sin_c)], axis=-1)
    q = axial(rms_norm(q, g_q)).reshape(B, S, C_KV_HEADS, H // C_KV_HEADS, d)
    k = axial(rms_norm(k, g_k))
    scale = d ** -0.5
    def block(qs, start):
        s = jnp.einsum('bqhgd,bkhd->bhgqk', qs[0], k).astype(jnp.float32) * scale
        p = jax.nn.softmax(s, axis=-1).astype(v.dtype)
        return jnp.einsum('bhgqk,bkhd->bqhgd', p, v)
    o = sweep_query_blocks(block, (q,))
    return o.reshape(B, S, H * d)


def diff_attn(q, k, v, lam_p, g_sub, lambda_init, table_d):
    B, S, H, _, d = q.shape
    lp = lam_p.astype(jnp.float32)
    lam = jnp.exp(jnp.sum(lp[0] * lp[1])) - jnp.exp(jnp.sum(lp[2] * lp[3])) + lambda_init
    kpos = jnp.arange(k.shape[1])
    tab = table_d.astype(jnp.float32)
    scale = d ** -0.5
    def block(qs, start):
        s = jnp.einsum('bqhcd,bkhcd->bhcqk', qs[0], k).astype(jnp.float32) * scale
        qpos = start + jnp.arange(QB)
        bias = tab[t5_bucket(kpos[None, :] - qpos[:, None])]
        s = s + jnp.transpose(bias, (2, 0, 1))[None, :, None]
        p = jax.nn.softmax(s, axis=-1)
        a = p[:, :, 0] - lam * p[:, :, 1]
        return jnp.einsum('bhqk,bkhe->bqhe', a.astype(v.dtype), v)
    o = sweep_query_blocks(block, (q,))
    o = rms_norm(o, g_sub) * (1.0 - lambda_init)
    return o.reshape(B, S, H * 2 * d)


def memory_cross_attn(h, mem_n, wq, wkv, wo):
    B, S, _ = h.shape
    M = mem_n.shape[1]
    q = (h @ wq).reshape(B, S, X_HEADS, X_HEAD_DIM)
    kv = (mem_n @ wkv).reshape(B, M, 2, X_HEADS, X_HEAD_DIM)
    s = jnp.einsum('bqhd,bkhd->bhqk', q, kv[:, :, 0]).astype(jnp.float32) * (X_HEAD_DIM ** -0.5)
    p = jax.nn.softmax(s, axis=-1).astype(h.dtype)
    o = jnp.einsum('bhqk,bkhd->bqhd', p, kv[:, :, 1])
    return o.reshape(B, S, X_HEADS * X_HEAD_DIM) @ wo


def expert_choice_ffn(h, w_router, w_gate, w_up, w_down):
    B, S, D = h.shape
    T = B * S
    cap = EC_CAPACITY * T // N_EXPERTS
    hf = h.reshape(T, D)
    aff = jax.nn.softmax((hf @ w_router).astype(jnp.float32), axis=-1)
    g, idx = lax.top_k(aff.T, cap)
    xe = hf[idx]
    a = jnp.einsum('ecd,edf->ecf', xe, w_gate)
    u = jnp.einsum('ecd,edf->ecf', xe, w_up)
    y = jnp.einsum('ecf,efd->ecd', jax.nn.silu(a) * u, w_down)
    y = y * g[..., None].astype(y.dtype)
    out = jnp.zeros_like(hf).at[idx.reshape(-1)].add(y.reshape(-1, D))
    return out.reshape(B, S, D)


def encoder_trunk(x, mem, p):
    B, S, _ = x.shape
    t = jnp.arange(S)
    rows = t // GRID_W
    cols = t % GRID_W
    cos_t, sin_t = rope_table(t, B_ROPE)
    cos_r, sin_r = rope_table(rows, HEAD_DIM // 2)
    cos_c, sin_c = rope_table(cols, HEAD_DIM // 2)
    table = p['rel_bias'].astype(jnp.float32)
    rel_win = jnp.arange(3 * QB)[None, :] - QB - jnp.arange(QB)[:, None]
    a_bias = jnp.transpose(table[t5_bucket(rel_win)][..., :A_HEADS], (2, 0, 1))
    table_d = p['rel_bias'][:, A_HEADS:]
    for layer in range(DEPTH):
        i = layer // 2
        h = rms_norm(x, p['norm_mix'][layer])
        if layer % 2 == 0:
            qa, ka, va, cq, ckv, kr = split_cols(h @ p['w_in_even'][i], EVEN_SIZES)
            o_a = window_gqa_sink(qa.reshape(B, S, A_HEADS, HEAD_DIM),
                                  ka.reshape(B, S, A_KV_HEADS, HEAD_DIM),
                                  va.reshape(B, S, A_KV_HEADS, HEAD_DIM),
                                  p['a_sink'][i], a_bias)
            o_b = mla(cq, ckv, kr, p['b_q_norm'][i], p['b_kv_norm'][i],
                      p['b_w_uq'][i], p['b_w_ukv'][i], cos_t, sin_t)
            x = x + jnp.concatenate([o_a, o_b], axis=-1) @ p['w_out_even'][i]
        else:
            qc, kc, vc, qd, kd, vd = split_cols(h @ p['w_in_odd'][i], ODD_SIZES)
            o_c = axial_gqa(qc.reshape(B, S, C_HEADS, HEAD_DIM),
                            kc.reshape(B, S, C_KV_HEADS, HEAD_DIM),
                            vc.reshape(B, S, C_KV_HEADS, HEAD_DIM),
                            p['c_q_norm'][i], p['c_k_norm'][i], cos_r, sin_r, cos_c, sin_c)
            lambda_init = 0.8 - 0.6 * math.exp(-0.3 * layer)
            o_d = diff_attn(qd.reshape(B, S, D_HEADS, 2, HEAD_DIM),
                            kd.reshape(B, S, D_HEADS, 2, HEAD_DIM),
                            vd.reshape(B, S, D_HEADS, 2 * HEAD_DIM),
                            p['d_lambda'][i], p['d_subln'][i], lambda_init, table_d)
            x = x + jnp.concatenate([o_c, o_d], axis=-1) @ p['w_out_odd'][i]
        h = rms_norm(x, p['norm_cross'][layer])
        m = rms_norm(mem, p['norm_mem'][layer])
        x = x + memory_cross_attn(h, m, p['x_wq'][layer], p['x_wkv'][layer], p['x_wo'][layer])
        h = rms_norm(x, p['norm_ffn'][layer])
        x = x + expert_choice_ffn(h, p['router'][layer], p['e_w_gate'][layer],
                                  p['e_w_up'][layer], p['e_w_down'][layer])
    return rms_norm(x, p['norm_final'])


def setup_inputs(seed: int = 0) -> dict:
    key = jax.random.key(seed)
    ks = iter(jax.random.split(key, 40))
    def nrm(shape, scale):
        return jax.random.normal(next(ks), shape, dtype=jnp.float32) * scale
    def gain(shape):
        return 1.0 + nrm(shape, 0.02)
    D = D_MODEL
    return {
        'x_prompt': nrm((BATCH, SEQ, D), 1.0),
        'x_sample': nrm((DEC_BATCH, DEC_SEQ, D), 1.0),
        'mem_prompt': nrm((BATCH, N_MEM, D), 1.0),
        'mem_sample': nrm((DEC_BATCH, N_MEM, D), 1.0),
        'rel_bias': nrm((N_BUCKETS, A_HEADS + D_HEADS), 0.5),
        'norm_mix': gain((DEPTH, D)),
        'norm_cross': gain((DEPTH, D)),
        'norm_mem': gain((DEPTH, D)),
        'norm_ffn': gain((DEPTH, D)),
        'norm_final': gain((D,)),
        'w_in_even': nrm((N_EVEN, D, EVEN_IN), D ** -0.5),
        'a_sink': nrm((N_EVEN, A_HEADS), 0.5),
        'b_q_norm': gain((N_EVEN, B_Q_LORA)),
        'b_kv_norm': gain((N_EVEN, B_KV_LORA)),
        'b_w_uq': nrm((N_EVEN, B_Q_LORA, B_HEADS * (B_NOPE + B_ROPE)), B_Q_LORA ** -0.5),
        'b_w_ukv': nrm((N_EVEN, B_KV_LORA, B_HEADS * (B_NOPE + B_V)), B_KV_LORA ** -0.5),
        'w_out_even': nrm((N_EVEN, MIX_OUT, D), MIX_OUT ** -0.5),
        'w_in_odd': nrm((N_ODD, D, ODD_IN), D ** -0.5),
        'c_q_norm': gain((N_ODD, HEAD_DIM)),
        'c_k_norm': gain((N_ODD, HEAD_DIM)),
        'd_lambda': nrm((N_ODD, 4, HEAD_DIM), 0.1),
        'd_subln': gain((N_ODD, 2 * HEAD_DIM)),
        'w_out_odd': nrm((N_ODD, MIX_OUT, D), MIX_OUT ** -0.5),
        'x_wq': nrm((DEPTH, D, X_HEADS * X_HEAD_DIM), D ** -0.5),
        'x_wkv': nrm((DEPTH, D, 2 * X_HEADS * X_HEAD_DIM), D ** -0.5),
        'x_wo': nrm((DEPTH, X_HEADS * X_HEAD_DIM, D), (X_HEADS * X_HEAD_DIM) ** -0.5),
        'router': nrm((DEPTH, D, N_EXPERTS), D ** -0.5),
        'e_w_gate': nrm((DEPTH, N_EXPERTS, D, EXPERT_FF), D ** -0.5),
        'e_w_up': nrm((DEPTH, N_EXPERTS, D, EXPERT_FF), D ** -0.5),
        'e_w_down': nrm((DEPTH, N_EXPERTS, EXPERT_FF, D), EXPERT_FF ** -0.5),
    }


def reference(x_prompt, x_sample, mem_prompt, mem_sample, rel_bias, norm_mix, norm_cross,
              norm_mem, norm_ffn, norm_final, w_in_even, a_sink, b_q_norm, b_kv_norm, b_w_uq,
              b_w_ukv, w_out_even, w_in_odd, c_q_norm, c_k_norm, d_lambda, d_subln, w_out_odd,
              x_wq, x_wkv, x_wo, router, e_w_gate, e_w_up, e_w_down):
    p = dict(rel_bias=rel_bias, norm_mix=norm_mix, norm_cross=norm_cross, norm_mem=norm_mem,
             norm_ffn=norm_ffn, norm_final=norm_final, w_in_even=w_in_even, a_sink=a_sink,
             b_q_norm=b_q_norm, b_kv_norm=b_kv_norm, b_w_uq=b_w_uq, b_w_ukv=b_w_ukv,
             w_out_even=w_out_even, w_in_odd=w_in_odd, c_q_norm=c_q_norm, c_k_norm=c_k_norm,
             d_lambda=d_lambda, d_subln=d_subln, w_out_odd=w_out_odd, x_wq=x_wq, x_wkv=x_wkv,
             x_wo=x_wo, router=router, e_w_gate=e_w_gate, e_w_up=e_w_up, e_w_down=e_w_down)
    y_prompt = encoder_trunk(x_prompt, mem_prompt, p)
    y_sample = encoder_trunk(x_sample, mem_sample, p)
    return (y_prompt, y_sample)
```

```python
import functools
import math

import jax
import jax.numpy as jnp
from jax import lax
from jax.experimental import pallas as pl
from jax.experimental.pallas import tpu as pltpu

F32 = jnp.float32
BF16 = jnp.bfloat16

D_MODEL = 1024
DEPTH = 4
HEAD_DIM = 64
QB = 128
WINDOW = 128
A_HEADS = 8
A_KV_HEADS = 2
B_HEADS = 8
B_Q_LORA = 384
B_KV_LORA = 256
B_NOPE = 64
B_ROPE = 32
B_V = 64
C_HEADS = 8
C_KV_HEADS = 2
D_HEADS = 4
X_HEADS = 4
X_HEAD_DIM = D_MODEL // X_HEADS
N_BUCKETS = 32
MAX_DISTANCE = 128
N_EXPERTS = 16
EXPERT_FF = 2 * D_MODEL
EC_CAPACITY = 2
GRID_W = 64
ROPE_BASE = 10000.0
NEG_INF = -1e30
EPS = 1e-6

LANES = 128
VMEM_LIMIT = 48 * 1024 * 1024

_NT = (((1,), (1,)), ((), ()))


def _cparams():
    return pltpu.CompilerParams(vmem_limit_bytes=VMEM_LIMIT)


def _rms(x, g):
    ms = jnp.mean(x * x, axis=-1, keepdims=True)
    return x * lax.rsqrt(ms + EPS) * g


def _rope(x, tab_ref):
    w = x.shape[1]
    return (x * tab_ref[0]
            + pltpu.roll(x, w - 16, 1) * tab_ref[1]
            + pltpu.roll(x, 16, 1) * tab_ref[2])


def _softmax_rows(s):
    m = jnp.max(s, axis=-1, keepdims=True)
    e = jnp.exp(s - m)
    return e, jnp.sum(e, axis=-1, keepdims=True)


def _pre_even_kernel(x_ref, g_ref, win_ref, gq_ref, gkv_ref, wuq_ref, wukv_ref, tq_ref, tk_ref,
                     qa_ref, ka_ref, va_ref, qcat_ref, kcat_ref, vb_ref):
    hn = _rms(x_ref[...], g_ref[...]).astype(BF16)
    z = jnp.dot(hn, win_ref[...], preferred_element_type=F32)
    qa_ref[...] = (z[:, 0:512] * (HEAD_DIM ** -0.5)).astype(BF16)
    ka_ref[...] = z[:, 512:640].astype(BF16)
    va_ref[...] = z[:, 640:768].astype(BF16)
    cq = _rms(z[:, 768:1152], gq_ref[...]).astype(BF16)
    q = jnp.dot(cq, wuq_ref[...], preferred_element_type=F32)
    for m in range(4):
        qcat_ref[:, 256 * m:256 * (m + 1)] = _rope(q[:, 256 * m:256 * (m + 1)], tq_ref).astype(BF16)
    ckv = _rms(z[:, 1152:1408], gkv_ref[...]).astype(BF16)
    kv = jnp.dot(ckv, wukv_ref[...], preferred_element_type=F32)
    kr = _rope(z[:, 1408:1536], tk_ref).astype(BF16)
    for m in range(4):
        kcat_ref[:, 256 * m:256 * m + 128] = kv[:, 128 * m:128 * (m + 1)].astype(BF16)
        kcat_ref[:, 256 * m + 128:256 * (m + 1)] = kr
    vb_ref[...] = kv[:, 512:1024].astype(BF16)


def _pre_odd_kernel(x_ref, g_ref, win_ref, gqk_ref, ones_ref, tab_ref,
                    qc_ref, kc_ref, vc_ref, qd_ref, kd_ref, vd_ref):
    hn = _rms(x_ref[...], g_ref[...]).astype(BF16)
    z = jnp.dot(hn, win_ref[...], preferred_element_type=F32)
    qk = z[:, 0:640]
    ss = qk * qk
    hi = ss.astype(BF16)
    lo = (ss - hi.astype(F32)).astype(BF16)
    ms = (jnp.dot(hi, ones_ref[...], preferred_element_type=F32)
          + jnp.dot(lo, ones_ref[...], preferred_element_type=F32)) * (1.0 / HEAD_DIM)
    qkn = qk * lax.rsqrt(ms + EPS) * gqk_ref[...]
    for m in range(4):
        blk = _rope(qkn[:, 128 * m:128 * (m + 1)], tab_ref)
        qc_ref[:, 128 * m:128 * (m + 1)] = (blk * (HEAD_DIM ** -0.5)).astype(BF16)
    kc_ref[...] = _rope(qkn[:, 512:640], tab_ref).astype(BF16)
    vc_ref[...] = z[:, 640:768].astype(BF16)
    qd_ref[...] = (z[:, 768:1280] * (HEAD_DIM ** -0.5)).astype(BF16)
    kd_ref[...] = z[:, 1280:1792].astype(BF16)
    vd_ref[...] = z[:, 1792:2304].astype(BF16)


def _const_spec(shape):
    nd = len(shape)
    return pl.BlockSpec(shape, lambda *_: (0,) * nd)


def _pre_even(x, S, g, win, gq, gkv, wuq, wukv, tq, tk, tm):
    T = x.shape[0]
    ns = S // tm
    row = lambda w: pl.BlockSpec((tm, w), lambda i: (i, 0))
    tab = lambda w: pl.BlockSpec((3, tm, w), lambda i: (0, i % ns, 0))
    out_w = (512, 128, 128, 1024, 1024, 512)
    return pl.pallas_call(
        _pre_even_kernel,
        grid=(T // tm,),
        in_specs=[row(D_MODEL), _const_spec(g.shape), _const_spec(win.shape), _const_spec(gq.shape),
                  _const_spec(gkv.shape), _const_spec(wuq.shape), _const_spec(wukv.shape),
                  tab(256), tab(128)],
        out_specs=[row(w) for w in out_w],
        out_shape=[jax.ShapeDtypeStruct((T, w), BF16) for w in out_w],
        compiler_params=_cparams(),
        name="pre_even",
    )(x, g, win, gq, gkv, wuq, wukv, tq, tk)


def _pre_odd(x, S, g, win, gqk, ones, tab3, tm):
    T = x.shape[0]
    ns = S // tm
    row = lambda w: pl.BlockSpec((tm, w), lambda i: (i, 0))
    out_w = (512, 128, 128, 512, 512, 512)
    return pl.pallas_call(
        _pre_odd_kernel,
        grid=(T // tm,),
        in_specs=[row(D_MODEL), _const_spec(g.shape), _const_spec(win.shape), _const_spec(gqk.shape),
                  _const_spec(ones.shape), pl.BlockSpec((3, tm, 128), lambda i: (0, i % ns, 0))],
        out_specs=[row(w) for w in out_w],
        out_shape=[jax.ShapeDtypeStruct((T, w), BF16) for w in out_w],
        compiler_params=_cparams(),
        name="pre_odd",
    )(x, g, win, gqk, ones, tab3)


def _window_kernel(q_ref, kp_ref, kc_ref, kn_ref, vp_ref, vc_ref, vn_ref, bias_ref, sink_ref, o_ref):
    i = pl.program_id(1)
    nb = pl.num_programs(1)
    q = q_ref[...]
    kw = jnp.concatenate([kp_ref[...], kc_ref[...], kn_ref[...]], axis=0)
    vw = jnp.concatenate([vp_ref[...], vc_ref[...], vn_ref[...]], axis=0)
    col = lax.broadcasted_iota(jnp.int32, (1, 3 * QB), 1)
    outside = ((col < QB) & (i == 0)) | ((col >= 2 * QB) & (i == nb - 1))
    lane = lax.broadcasted_iota(jnp.int32, (1, LANES), 1)
    for m in range(4):
        qblk = q[:, 128 * m:128 * (m + 1)]
        outs = []
        for half in range(2):
            h = m + 4 * half
            keep = (lane < HEAD_DIM) if half == 0 else (lane >= HEAD_DIM)
            qm = jnp.where(keep, qblk, jnp.zeros_like(qblk))
            s = lax.dot_general(qm, kw, _NT, preferred_element_type=F32) + bias_ref[h]
            s = jnp.where(outside, NEG_INF, s)
            sk = sink_ref[h:h + 1, 0:1]
            mx = jnp.maximum(jnp.max(s, axis=-1, keepdims=True), sk)
            e = jnp.exp(s - mx)
            den = jnp.sum(e, axis=-1, keepdims=True) + jnp.exp(sk - mx)
            outs.append(jnp.dot(e.astype(BF16), vw, preferred_element_type=F32) / den)
        o_ref[:, 128 * m:128 * (m + 1)] = jnp.where(lane < HEAD_DIM, outs[0], outs[1]).astype(BF16)


def _window_attn(qa, ka, va, bias, sink, B, S):
    T = qa.shape[0]
    nb = S // QB
    kv_spec = lambda d: pl.BlockSpec(
        (QB, 128), lambda b, i: (b * nb + jnp.clip(i + d, 0, nb - 1), 0))
    return pl.pallas_call(
        _window_kernel,
        grid=(B, nb),
        in_specs=[pl.BlockSpec((QB, 512), lambda b, i: (b * nb + i, 0)),
                  kv_spec(-1), kv_spec(0), kv_spec(1), kv_spec(-1), kv_spec(0), kv_spec(1),
                  _const_spec(bias.shape), _const_spec(sink.shape)],
        out_specs=pl.BlockSpec((QB, 512), lambda b, i: (b * nb + i, 0)),
        out_shape=jax.ShapeDtypeStruct((T, 512), BF16),
        compiler_params=_cparams(),
        name="window_attn",
    )(qa, ka, ka, ka, va, va, va, bias, sink)


def _pair_kernel(q_ref, k_ref, v_ref, o_ref, *, lanes_a, lanes_b):
    q = q_ref[...]
    k = k_ref[...]
    v = v_ref[...]
    wq = q.shape[1]
    lane = lax.broadcasted_iota(jnp.int32, (1, wq), 1)
    outs = []
    for ranges in (lanes_a, lanes_b):
        keep = functools.reduce(jnp.logical_or, [(lane >= lo) & (lane < hi) for lo, hi in ranges])
        qm = jnp.where(keep, q, jnp.zeros_like(q))
        s = lax.dot_general(qm, k, _NT, preferred_element_type=F32)
        e, l = _softmax_rows(s)
        outs.append(jnp.dot(e.astype(BF16), v, preferred_element_type=F32) / l)
    lane_o = lax.broadcasted_iota(jnp.int32, (1, LANES), 1)
    o_ref[...] = jnp.where(lane_o < 64, outs[0], outs[1]).astype(BF16)


def _pair_attn(q, k, v, B, S, tq, wq, shared_kv, lanes_a, lanes_b, name):
    T = q.shape[0]
    nq = S // tq
    kv_map = (lambda b, m, i: (b, 0)) if shared_kv else (lambda b, m, i: (b, m))
    return pl.pallas_call(
        functools.partial(_pair_kernel, lanes_a=lanes_a, lanes_b=lanes_b),
        grid=(B, 4, nq),
        in_specs=[pl.BlockSpec((tq, wq), lambda b, m, i: (b * nq + i, m)),
                  pl.BlockSpec((S, wq), kv_map),
                  pl.BlockSpec((S, 128), kv_map)],
        out_specs=pl.BlockSpec((tq, 128), lambda b, m, i: (b * nq + i, m)),
        out_shape=jax.ShapeDtypeStruct((T, 512), BF16),
        compiler_params=_cparams(),
        name=name,
    )(q, k, v)


def _diff_kernel(lam_ref, gsub_ref, q_ref, k_ref, v_ref, bias_ref, o_ref, s_scr, *, S, tq, lambda_init):
    i = pl.program_id(2)
    r = tq // QB
    q = q_ref[...]
    k = k_ref[...]
    lp = lam_ref[...]
    lam = (jnp.exp(jnp.sum(lp[0:1] * lp[1:2], axis=-1, keepdims=True))
           - jnp.exp(jnp.sum(lp[2:3] * lp[3:4], axis=-1, keepdims=True)) + lambda_init)
    lane = lax.broadcasted_iota(jnp.int32, (1, LANES), 1)
    probs = []
    for c in range(2):
        keep = (lane < HEAD_DIM) if c == 0 else (lane >= HEAD_DIM)
        qm = jnp.where(keep, q, jnp.zeros_like(q))
        s_scr[...] = lax.dot_general(qm, k, _NT, preferred_element_type=F32)
        for j in range(S // QB):
            u = jnp.clip(j - r * i + 2, 0, r + 3)
            s_scr[:, QB * j:QB * (j + 1)] = s_scr[:, QB * j:QB * (j + 1)] + bias_ref[0, u]
        e, l = _softmax_rows(s_scr[...])
        probs.append(e * (1.0 / l))
    a = (probs[0] - lam * probs[1]).astype(BF16)
    o = jnp.dot(a, v_ref[...], preferred_element_type=F32)
    o_ref[...] = (_rms(o, gsub_ref[...]) * (1.0 - lambda_init)).astype(BF16)


def _diff_attn(qd, kd, vd, lam_p, gsub, bias, B, S, tq, lambda_init):
    T = qd.shape[0]
    nq = S // tq
    nt = bias.shape[1]
    return pl.pallas_call(
        functools.partial(_diff_kernel, S=S, tq=tq, lambda_init=lambda_init),
        grid=(B, D_HEADS, nq),
        in_specs=[_const_spec(lam_p.shape), _const_spec(gsub.shape),
                  pl.BlockSpec((tq, 128), lambda b, h, i: (b * nq + i, h)),
                  pl.BlockSpec((S, 128), lambda b, h, i: (b, h)),
                  pl.BlockSpec((S, 128), lambda b, h, i: (b, h)),
                  pl.BlockSpec((1, nt, tq, 128), lambda b, h, i: (h, 0, 0, 0))],
        out_specs=pl.BlockSpec((tq, 128), lambda b, h, i: (b * nq + i, h)),
        out_shape=jax.ShapeDtypeStruct((T, 512), BF16),
        scratch_shapes=[pltpu.VMEM((tq, S), F32)],
        compiler_params=_cparams(),
        name="diff_attn",
    )(lam_p, gsub, qd, kd, vd, bias)


def _norm_matmul_kernel(x_ref, g_ref, w_ref, o_ref):
    hn = _rms(x_ref[...], g_ref[...]).astype(BF16)
    o_ref[...] = jnp.dot(hn, w_ref[...], preferred_element_type=F32).astype(o_ref.dtype)


def _norm_matmul(x, g, w, tm, out_dtype):
    T, K = x.shape
    N = w.shape[1]
    return pl.pallas_call(
        _norm_matmul_kernel,
        grid=(T // tm,),
        in_specs=[pl.BlockSpec((tm, K), lambda i: (i, 0)), _const_spec(g.shape), _const_spec(w.shape)],
        out_specs=pl.BlockSpec((tm, N), lambda i: (i, 0)),
        out_shape=jax.ShapeDtypeStruct((T, N), out_dtype),
        compiler_params=_cparams(),
        name="norm_matmul",
    )(x, g, w)


def _post_kernel(x_ref, oa_ref, ob_ref, wout_ref, gx_ref, wq_ref, kv_ref, wo_ref, gf_ref, wr_ref,
                 x2_ref, hn_ref, aff_ref):
    x1 = (x_ref[...]
          + jnp.dot(oa_ref[...], wout_ref[0:512, :], preferred_element_type=F32)
          + jnp.dot(ob_ref[...], wout_ref[512:1024, :], preferred_element_type=F32))
    hc = _rms(x1, gx_ref[...]).astype(BF16)
    q = (jnp.dot(hc, wq_ref[...], preferred_element_type=F32) * (X_HEAD_DIM ** -0.5)).astype(BF16)
    heads = []
    for h in range(X_HEADS):
        lo = X_HEAD_DIM * h
        kh = kv_ref[:, lo:lo + X_HEAD_DIM]
        vh = kv_ref[:, D_MODEL + lo:D_MODEL + lo + X_HEAD_DIM]
        s = lax.dot_general(q[:, lo:lo + X_HEAD_DIM], kh, _NT, preferred_element_type=F32)
        e, l = _softmax_rows(s)
        heads.append((jnp.dot(e.astype(BF16), vh, preferred_element_type=F32) / l).astype(BF16))
    o = jnp.concatenate(heads, axis=1)
    x2 = x1 + jnp.dot(o, wo_ref[...], preferred_element_type=F32)
    x2_ref[...] = x2
    hf = _rms(x2, gf_ref[...])
    hn_ref[...] = hf.astype(BF16)
    h1 = hf.astype(BF16)
    r1 = hf - h1.astype(F32)
    h2 = r1.astype(BF16)
    h3 = (r1 - h2.astype(F32)).astype(BF16)
    wr = wr_ref[...]
    logits = jnp.zeros((N_EXPERTS, hf.shape[0]), F32)
    for a, hpart in enumerate((h1, h2, h3)):
        for b in range(3 - a):
            logits = logits + lax.dot_general(wr[b], hpart, _NT, preferred_element_type=F32)
    mx = jnp.max(logits, axis=0, keepdims=True)
    e = jnp.exp(logits - mx)
    aff_ref[...] = e / jnp.sum(e, axis=0, keepdims=True)


def _post(x, oa, ob, wout, gx, wq, kv, wo, gf, wr3, S, n_mem, tm):
    T = x.shape[0]
    per_b = S // tm
    row = lambda w: pl.BlockSpec((tm, w), lambda i: (i, 0))
    return pl.pallas_call(
        _post_kernel,
        grid=(T // tm,),
        in_specs=[row(D_MODEL), row(512), row(512), _const_spec(wout.shape), _const_spec(gx.shape),
                  _const_spec(wq.shape), pl.BlockSpec((n_mem, 2 * D_MODEL), lambda i: (i // per_b, 0)),
                  _const_spec(wo.shape), _const_spec(gf.shape), _const_spec(wr3.shape)],
        out_specs=[row(D_MODEL), row(D_MODEL), pl.BlockSpec((N_EXPERTS, tm), lambda i: (0, i))],
        out_shape=[jax.ShapeDtypeStruct((T, D_MODEL), F32), jax.ShapeDtypeStruct((T, D_MODEL), BF16),
                   jax.ShapeDtypeStruct((N_EXPERTS, T), F32)],
        compiler_params=_cparams(),
        name="post",
    )(x, oa, ob, wout, gx, wq, kv, wo, gf, wr3)


def _ffn_kernel(xe_ref, wg_ref, wu_ref, wd_ref, g_ref, y_ref):
    f = pl.program_id(2)
    xe = xe_ref[0]
    a = jnp.dot(xe, wg_ref[0].astype(BF16), preferred_element_type=F32)
    u = jnp.dot(xe, wu_ref[0].astype(BF16), preferred_element_type=F32)
    h = (a * (1.0 / (1.0 + jnp.exp(-a))) * u).astype(BF16)
    part = jnp.dot(h, wd_ref[0].astype(BF16), preferred_element_type=F32)

    @pl.when(f == 0)
    def _():
        y_ref[0] = part

    @pl.when(f > 0)
    def _():
        y_ref[0] = y_ref[0] + part

    @pl.when(f == pl.num_programs(2) - 1)
    def _():
        y_ref[0] = y_ref[0] * g_ref[0]


def _expert_ffn(xe, wg, wu, wd, g, tc, tf):
    E, cap, D = xe.shape
    F = wg.shape[2]
    return pl.pallas_call(
        _ffn_kernel,
        grid=(E, cap // tc, F // tf),
        in_specs=[pl.BlockSpec((1, tc, D), lambda e, c, f: (e, c, 0)),
                  pl.BlockSpec((1, D, tf), lambda e, c, f: (e, 0, f)),
                  pl.BlockSpec((1, D, tf), lambda e, c, f: (e, 0, f)),
                  pl.BlockSpec((1, tf, D), lambda e, c, f: (e, f, 0)),
                  pl.BlockSpec((1, tc, 1), lambda e, c, f: (e, c, 0))],
        out_specs=pl.BlockSpec((1, tc, D), lambda e, c, f: (e, c, 0)),
        out_shape=jax.ShapeDtypeStruct((E, cap, D), F32),
        compiler_params=_cparams(),
        name="expert_ffn",
    )(xe, wg, wu, wd, g)


def _final_norm_kernel(x_ref, g_ref, o_ref):
    o_ref[...] = _rms(x_ref[...], g_ref[...])


def _final_norm(x, g, tm):
    T, D = x.shape
    return pl.pallas_call(
        _final_norm_kernel,
        grid=(T // tm,),
        in_specs=[pl.BlockSpec((tm, D), lambda i: (i, 0)), _const_spec(g.shape)],
        out_specs=pl.BlockSpec((tm, D), lambda i: (i, 0)),
        out_shape=jax.ShapeDtypeStruct((T, D), F32),
        name="final_norm",
    )(x, g)


def _t5_bucket(rel):
    half = N_BUCKETS // 2
    exact = half // 2
    n = jnp.abs(rel)
    big = exact + (jnp.log(jnp.maximum(n, 1).astype(F32) / exact)
                   / math.log(MAX_DISTANCE / exact) * (half - exact)).astype(jnp.int32)
    big = jnp.minimum(big, half - 1)
    return jnp.where(rel > 0, half, 0) + jnp.where(n < exact, n, big)


def _rope_freqs(pos, dim):
    freqs = ROPE_BASE ** (-jnp.arange(0, dim, 2, dtype=F32) / dim)
    ang = pos.astype(F32)[:, None] * freqs[None, :]
    return jnp.cos(ang), jnp.sin(ang)


def _rope_lane_table(cos, sin, active):
    w = cos.shape[1]
    first = (jnp.arange(w) % 32) < 16
    c = jnp.where(active[None, :], cos, 1.0)
    s1 = jnp.where((active & first)[None, :], -sin, 0.0)
    s2 = jnp.where((active & ~first)[None, :], sin, 0.0)
    return jnp.stack([c, s1, s2]).astype(F32)


def _head_pair_perm():
    cols = []
    for m in range(4):
        cols.append(jnp.arange(64 * m, 64 * (m + 1)))
        cols.append(jnp.arange(64 * (4 + m), 64 * (5 + m)))
    return jnp.concatenate(cols)


def _tables(S, tq_diff, rel_bias):
    t = jnp.arange(S)
    lane128 = jnp.arange(128)
    lane256 = jnp.arange(256)
    cos_t, sin_t = _rope_freqs(t, B_ROPE)
    scale_b = (B_NOPE + B_ROPE) ** -0.5
    tq = _rope_lane_table(cos_t[:, lane256 % 16], sin_t[:, lane256 % 16],
                          (lane256 >= 128) & (lane256 < 192)) * scale_b
    tk = _rope_lane_table(cos_t[:, lane128 % 16], sin_t[:, lane128 % 16], lane128 < 64)
    cos_r, sin_r = _rope_freqs(t // GRID_W, HEAD_DIM // 2)
    cos_c, sin_c = _rope_freqs(t % GRID_W, HEAD_DIM // 2)
    is_row = ((lane128 % 64) < 32)[None, :]
    cos_ax = jnp.where(is_row, cos_r[:, lane128 % 16], cos_c[:, lane128 % 16])
    sin_ax = jnp.where(is_row, sin_r[:, lane128 % 16], sin_c[:, lane128 % 16])
    tax = _rope_lane_table(cos_ax, sin_ax, jnp.ones((128,), bool))
    table = rel_bias.astype(F32)
    rel_win = jnp.arange(3 * QB)[None, :] - QB - jnp.arange(QB)[:, None]
    a_bias = jnp.transpose(table[_t5_bucket(rel_win)][..., :A_HEADS], (2, 0, 1))
    a_bias = jnp.where((jnp.abs(rel_win) <= WINDOW)[None], a_bias, NEG_INF)
    r = tq_diff // QB
    rel_d = (QB * (jnp.arange(r + 4)[:, None, None] - 2)
             + jnp.arange(QB)[None, None, :] - jnp.arange(tq_diff)[None, :, None])
    d_bias = jnp.transpose(table[:, A_HEADS:][_t5_bucket(rel_d)], (3, 0, 1, 2))
    return dict(tq=tq, tk=tk, tax=tax, a_bias=a_bias, d_bias=d_bias)


def _prep_weights(p):
    perm = _head_pair_perm()
    row = lambda v: v.reshape(1, -1).astype(F32)
    L = {}
    L['norm_mix'] = [row(p['norm_mix'][l]) for l in range(DEPTH)]
    L['norm_cross'] = [row(p['norm_cross'][l]) for l in range(DEPTH)]
    L['norm_mem'] = [row(p['norm_mem'][l]) for l in range(DEPTH)]
    L['norm_ffn'] = [row(p['norm_ffn'][l]) for l in range(DEPTH)]
    even, odd = [], []
    for i in range((DEPTH + 1) // 2):
        w = p['w_in_even'][i]
        kr = w[:, 1408:1440]
        win = jnp.concatenate([w[:, 0:512][:, perm], w[:, 512:1408], kr, kr,
                               jnp.zeros((D_MODEL, 64), w.dtype)], axis=1).astype(BF16)
        uq = p['b_w_uq'][i].reshape(B_Q_LORA, B_HEADS, B_NOPE + B_ROPE)
        blocks = []
        for m in range(4):
            blocks += [uq[:, 2 * m, :B_NOPE], uq[:, 2 * m + 1, :B_NOPE],
                       uq[:, 2 * m, B_NOPE:], uq[:, 2 * m + 1, B_NOPE:],
                       jnp.zeros((B_Q_LORA, 64), uq.dtype)]
        wuq = jnp.concatenate(blocks, axis=1).astype(BF16)
        ukv = p['b_w_ukv'][i].reshape(B_KV_LORA, B_HEADS, B_NOPE + B_V)
        wukv = jnp.concatenate([ukv[:, :, :B_NOPE].reshape(B_KV_LORA, -1),
                                ukv[:, :, B_NOPE:].reshape(B_KV_LORA, -1)], axis=1).astype(BF16)
        wo = p['w_out_even'][i]
        wout = jnp.concatenate([wo[0:512][perm], wo[512:1024]], axis=0).astype(BF16)
        sink = jnp.broadcast_to(p['a_sink'][i].astype(F32)[:, None], (A_HEADS, LANES))
        even.append(dict(win=win, wuq=wuq, wukv=wukv, wout=wout, sink=sink,
                         gq=row(p['b_q_norm'][i]), gkv=row(p['b_kv_norm'][i])))
    for i in range(DEPTH // 2):
        w = p['w_in_odd'][i]
        win = jnp.concatenate([w[:, 0:512][:, perm], w[:, 512:]], axis=1).astype(BF16)
        gqk = jnp.concatenate([jnp.tile(p['c_q_norm'][i], C_HEADS),
                               jnp.tile(p['c_k_norm'][i], C_KV_HEADS)]).reshape(1, -1).astype(F32)
        wo = p['w_out_odd'][i]
        wout = jnp.concatenate([wo[0:512][perm], wo[512:1024]], axis=0).astype(BF16)
        odd.append(dict(win=win, gqk=gqk, wout=wout, lam=p['d_lambda'][i].astype(F32),
                        gsub=row(p['d_subln'][i])))
    L['even'], L['odd'] = even, odd
    seg = jnp.arange(640) // HEAD_DIM
    L['ones'] = (seg[:, None] == seg[None, :]).astype(BF16)
    L['x_wq'] = [p['x_wq'][l].astype(BF16) for l in range(DEPTH)]
    L['x_wkv'] = [p['x_wkv'][l].astype(BF16) for l in range(DEPTH)]
    L['x_wo'] = [p['x_wo'][l].astype(BF16) for l in range(DEPTH)]
    wr3 = []
    for l in range(DEPTH):
        wt = p['router'][l].T.astype(F32)
        w1 = wt.astype(BF16)
        r1 = wt - w1.astype(F32)
        w2 = r1.astype(BF16)
        w3 = (r1 - w2.astype(F32)).astype(BF16)
        wr3.append(jnp.stack([w1, w2, w3]))
    L['router'] = wr3
    L['norm_final'] = row(p['norm_final'])
    return L


def _tile(n, pref):
    t = min(n, pref)
    assert n % t == 0, (n, t)
    return t


def _trunk(x, mem, p, L):
    B, S, D = x.shape
    n_mem = mem.shape[1]
    T = B * S
    tm_pre = _tile(S, 512)
    tm_post = _tile(S, 256)
    tq = _tile(S, 256)
    cap = EC_CAPACITY * T // N_EXPERTS
    tc = _tile(cap, 1024)
    tf = 512
    tabs = _tables(S, tq, p['rel_bias'])
    xf = x.reshape(T, D).astype(F32)
    memf = mem.reshape(B * n_mem, D).astype(F32)
    tm_mem = _tile(B * n_mem, 256)
    mla_a = ((0, 64), (128, 160))
    mla_b = ((64, 128), (160, 192))
    for layer in range(DEPTH):
        i = layer // 2
        if layer % 2 == 0:
            w = L['even'][i]
            qa, ka, va, qcat, kcat, vb = _pre_even(xf, S, L['norm_mix'][layer], w['win'], w['gq'], w['gkv'],
                                                   w['wuq'], w['wukv'], tabs['tq'], tabs['tk'], tm_pre)
            o1 = _window_attn(qa, ka, va, tabs['a_bias'], w['sink'], B, S)
            o2 = _pair_attn(qcat, kcat, vb, B, S, tq, 256, False, mla_a, mla_b, "latent_attn")
        else:
            w = L['odd'][i]
            qc, kc, vc, qd, kd, vd = _pre_odd(xf, S, L['norm_mix'][layer], w['win'], w['gqk'], L['ones'],
                                              tabs['tax'], tm_pre)
            o1 = _pair_attn(qc, kc, vc, B, S, tq, 128, True, ((0, 64),), ((64, 128),), "axial_attn")
            lambda_init = 0.8 - 0.6 * math.exp(-0.3 * layer)
            o2 = _diff_attn(qd, kd, vd, w['lam'], w['gsub'], tabs['d_bias'], B, S, tq, lambda_init)
        kv = _norm_matmul(memf, L['norm_mem'][layer], L['x_wkv'][layer], tm_mem, BF16)
        x2, hn, aff = _post(xf, o1, o2, w['wout'], L['norm_cross'][layer], L['x_wq'][layer], kv,
                            L['x_wo'][layer], L['norm_ffn'][layer], L['router'][layer], S, n_mem, tm_post)
        g, idx = lax.top_k(aff, cap)
        xe = hn[idx]
        y = _expert_ffn(xe, p['e_w_gate'][layer], p['e_w_up'][layer], p['e_w_down'][layer],
                        g[..., None], tc, tf)
        xf = x2.at[idx.reshape(-1)].add(y.reshape(-1, D))
    out = _final_norm(xf, L['norm_final'], tm_pre)
    return out.reshape(B, S, D)


def kernel(x_prompt, x_sample, mem_prompt, mem_sample, rel_bias, norm_mix, norm_cross, norm_mem, norm_ffn,
           norm_final, w_in_even, a_sink, b_q_norm, b_kv_norm, b_w_uq, b_w_ukv, w_out_even, w_in_odd,
           c_q_norm, c_k_norm, d_lambda, d_subln, w_out_odd, x_wq, x_wkv, x_wo, router, e_w_gate,
           e_w_up, e_w_down):
    p = dict(rel_bias=rel_bias, norm_mix=norm_mix, norm_cross=norm_cross, norm_mem=norm_mem,
             norm_ffn=norm_ffn, norm_final=norm_final, w_in_even=w_in_even, a_sink=a_sink,
             b_q_norm=b_q_norm, b_kv_norm=b_kv_norm, b_w_uq=b_w_uq, b_w_ukv=b_w_ukv,
             w_out_even=w_out_even, w_in_odd=w_in_odd, c_q_norm=c_q_norm, c_k_norm=c_k_norm,
             d_lambda=d_lambda, d_subln=d_subln, w_out_odd=w_out_odd, x_wq=x_wq, x_wkv=x_wkv,
             x_wo=x_wo, router=router, e_w_gate=e_w_gate, e_w_up=e_w_up, e_w_down=e_w_down)
    L = _prep_weights(p)
    y_prompt = _trunk(x_prompt, mem_prompt, p, L)
    y_sample = _trunk(x_sample, mem_sample, p, L)
    return (y_prompt, y_sample)
```

```python
import functools
import math

import jax
import jax.numpy as jnp
from jax import lax
from jax.experimental import pallas as pl
from jax.experimental.pallas import tpu as pltpu

F32 = jnp.float32
BF16 = jnp.bfloat16

D_MODEL = 1024
DEPTH = 4
HEAD_DIM = 64
QB = 128
WINDOW = 128
A_HEADS = 8
A_KV_HEADS = 2
B_HEADS = 8
B_Q_LORA = 384
B_KV_LORA = 256
B_NOPE = 64
B_ROPE = 32
B_V = 64
C_HEADS = 8
C_KV_HEADS = 2
D_HEADS = 4
X_HEADS = 4
X_HEAD_DIM = D_MODEL // X_HEADS
N_BUCKETS = 32
MAX_DISTANCE = 128
N_EXPERTS = 16
EXPERT_FF = 2 * D_MODEL
EC_CAPACITY = 2
GRID_W = 64
ROPE_BASE = 10000.0
NEG_INF = -1e30
EPS = 1e-6

LANES = 128
VMEM_LIMIT = 48 * 1024 * 1024

_NT = (((1,), (1,)), ((), ()))


def _cparams():
    return pltpu.CompilerParams(vmem_limit_bytes=VMEM_LIMIT)


def _rms(x, g):
    ms = jnp.mean(x * x, axis=-1, keepdims=True)
    return x * lax.rsqrt(ms + EPS) * g


def _rope(x, tab_ref):
    w = x.shape[1]
    return (x * tab_ref[0]
            + pltpu.roll(x, w - 16, 1) * tab_ref[1]
            + pltpu.roll(x, 16, 1) * tab_ref[2])


def _softmax_rows(s):
    m = jnp.max(s, axis=-1, keepdims=True)
    e = jnp.exp(s - m)
    return e, jnp.sum(e, axis=-1, keepdims=True)


def _pre_even_kernel(x_ref, g_ref, win_ref, gq_ref, gkv_ref, wuq_ref, wukv_ref, tq_ref, tk_ref,
                     qa_ref, ka_ref, va_ref, qcat_ref, kcat_ref, vb_ref):
    hn = _rms(x_ref[...], g_ref[...]).astype(BF16)
    z = jnp.dot(hn, win_ref[...], preferred_element_type=F32)
    qa_ref[...] = (z[:, 0:512] * (HEAD_DIM ** -0.5)).astype(BF16)
    ka_ref[...] = z[:, 512:640].astype(BF16)
    va_ref[...] = z[:, 640:768].astype(BF16)
    cq = _rms(z[:, 768:1152], gq_ref[...]).astype(BF16)
    q = jnp.dot(cq, wuq_ref[...], preferred_element_type=F32)
    for m in range(4):
        qcat_ref[:, 256 * m:256 * (m + 1)] = _rope(q[:, 256 * m:256 * (m + 1)], tq_ref).astype(BF16)
    ckv = _rms(z[:, 1152:1408], gkv_ref[...]).astype(BF16)
    kv = jnp.dot(ckv, wukv_ref[...], preferred_element_type=F32)
    kr = _rope(z[:, 1408:1536], tk_ref).astype(BF16)
    for m in range(4):
        kcat_ref[:, 256 * m:256 * m + 128] = kv[:, 128 * m:128 * (m + 1)].astype(BF16)
        kcat_ref[:, 256 * m + 128:256 * (m + 1)] = kr
    vb_ref[...] = kv[:, 512:1024].astype(BF16)


def _pre_odd_kernel(x_ref, g_ref, win_ref, gqk_ref, ones_ref, tab_ref,
                    qc_ref, kc_ref, vc_ref, qd_ref, kd_ref, vd_ref):
    hn = _rms(x_ref[...], g_ref[...]).astype(BF16)
    z = jnp.dot(hn, win_ref[...], preferred_element_type=F32)
    qk = z[:, 0:640]
    ss = qk * qk
    hi = ss.astype(BF16)
    lo = (ss - hi.astype(F32)).astype(BF16)
    ms = (jnp.dot(hi, ones_ref[...], preferred_element_type=F32)
          + jnp.dot(lo, ones_ref[...], preferred_element_type=F32)) * (1.0 / HEAD_DIM)
    qkn = qk * lax.rsqrt(ms + EPS) * gqk_ref[...]
    for m in range(4):
        blk = _rope(qkn[:, 128 * m:128 * (m + 1)], tab_ref)
        qc_ref[:, 128 * m:128 * (m + 1)] = (blk * (HEAD_DIM ** -0.5)).astype(BF16)
    kc_ref[...] = _rope(qkn[:, 512:640], tab_ref).astype(BF16)
    vc_ref[...] = z[:, 640:768].astype(BF16)
    qd_ref[...] = (z[:, 768:1280] * (HEAD_DIM ** -0.5)).astype(BF16)
    kd_ref[...] = z[:, 1280:1792].astype(BF16)
    vd_ref[...] = z[:, 1792:2304].astype(BF16)


def _const_spec(shape):
    nd = len(shape)
    return pl.BlockSpec(shape, lambda *_: (0,) * nd)


def _pre_even(x, S, g, win, gq, gkv, wuq, wukv, tq, tk, tm):
    T = x.shape[0]
    ns = S // tm
    row = lambda w: pl.BlockSpec((tm, w), lambda i: (i, 0))
    tab = lambda w: pl.BlockSpec((3, tm, w), lambda i: (0, i % ns, 0))
    out_w = (512, 128, 128, 1024, 1024, 512)
    return pl.pallas_call(
        _pre_even_kernel,
        grid=(T // tm,),
        in_specs=[row(D_MODEL), _const_spec(g.shape), _const_spec(win.shape), _const_spec(gq.shape),
                  _const_spec(gkv.shape), _const_spec(wuq.shape), _const_spec(wukv.shape),
                  tab(256), tab(128)],
        out_specs=[row(w) for w in out_w],
        out_shape=[jax.ShapeDtypeStruct((T, w), BF16) for w in out_w],
        compiler_params=_cparams(),
        name="pre_even",
    )(x, g, win, gq, gkv, wuq, wukv, tq, tk)


def _pre_odd(x, S, g, win, gqk, ones, tab3, tm):
    T = x.shape[0]
    ns = S // tm
    row = lambda w: pl.BlockSpec((tm, w), lambda i: (i, 0))
    out_w = (512, 128, 128, 512, 512, 512)
    return pl.pallas_call(
        _pre_odd_kernel,
        grid=(T // tm,),
        in_specs=[row(D_MODEL), _const_spec(g.shape), _const_spec(win.shape), _const_spec(gqk.shape),
                  _const_spec(ones.shape), pl.BlockSpec((3, tm, 128), lambda i: (0, i % ns, 0))],
        out_specs=[row(w) for w in out_w],
        out_shape=[jax.ShapeDtypeStruct((T, w), BF16) for w in out_w],
        compiler_params=_cparams(),
        name="pre_odd",
    )(x, g, win, gqk, ones, tab3)


def _window_kernel(q_ref, kp_ref, kc_ref, kn_ref, vp_ref, vc_ref, vn_ref, bias_ref, sink_ref, o_ref):
    i = pl.program_id(1)
    nb = pl.num_programs(1)
    q = q_ref[...]
    kw = jnp.concatenate([kp_ref[...], kc_ref[...], kn_ref[...]], axis=0)
    vw = jnp.concatenate([vp_ref[...], vc_ref[...], vn_ref[...]], axis=0)
    col = lax.broadcasted_iota(jnp.int32, (1, 3 * QB), 1)
    outside = ((col < QB) & (i == 0)) | ((col >= 2 * QB) & (i == nb - 1))
    lane = lax.broadcasted_iota(jnp.int32, (1, LANES), 1)
    for m in range(4):
        qblk = q[:, 128 * m:128 * (m + 1)]
        outs = []
        for half in range(2):
            h = m + 4 * half
            keep = (lane < HEAD_DIM) if half == 0 else (lane >= HEAD_DIM)
            qm = jnp.where(keep, qblk, jnp.zeros_like(qblk))
            s = lax.dot_general(qm, kw, _NT, preferred_element_type=F32) + bias_ref[h]
            s = jnp.where(outside, NEG_INF, s)
            sk = sink_ref[h:h + 1, 0:1]
            mx = jnp.maximum(jnp.max(s, axis=-1, keepdims=True), sk)
            e = jnp.exp(s - mx)
            den = jnp.sum(e, axis=-1, keepdims=True) + jnp.exp(sk - mx)
            outs.append(jnp.dot(e.astype(BF16), vw, preferred_element_type=F32) / den)
        o_ref[:, 128 * m:128 * (m + 1)] = jnp.where(lane < HEAD_DIM, outs[0], outs[1]).astype(BF16)


def _window_attn(qa, ka, va, bias, sink, B, S):
    T = qa.shape[0]
    nb = S // QB
    kv_spec = lambda d: pl.BlockSpec(
        (QB, 128), lambda b, i: (b * nb + jnp.clip(i + d, 0, nb - 1), 0))
    return pl.pallas_call(
        _window_kernel,
        grid=(B, nb),
        in_specs=[pl.BlockSpec((QB, 512), lambda b, i: (b * nb + i, 0)),
                  kv_spec(-1), kv_spec(0), kv_spec(1), kv_spec(-1), kv_spec(0), kv_spec(1),
                  _const_spec(bias.shape), _const_spec(sink.shape)],
        out_specs=pl.BlockSpec((QB, 512), lambda b, i: (b * nb + i, 0)),
        out_shape=jax.ShapeDtypeStruct((T, 512), BF16),
        compiler_params=_cparams(),
        name="window_attn",
    )(qa, ka, ka, ka, va, va, va, bias, sink)


def _pair_kernel(q_ref, k_ref, v_ref, o_ref, *, lanes_a, lanes_b):
    q = q_ref[...]
    k = k_ref[...]
    v = v_ref[...]
    wq = q.shape[1]
    lane = lax.broadcasted_iota(jnp.int32, (1, wq), 1)
    outs = []
    for ranges in (lanes_a, lanes_b):
        keep = functools.reduce(jnp.logical_or, [(lane >= lo) & (lane < hi) for lo, hi in ranges])
        qm = jnp.where(keep, q, jnp.zeros_like(q))
        s = lax.dot_general(qm, k, _NT, preferred_element_type=F32)
        e, l = _softmax_rows(s)
        outs.append(jnp.dot(e.astype(BF16), v, preferred_element_type=F32) / l)
    lane_o = lax.broadcasted_iota(jnp.int32, (1, LANES), 1)
    o_ref[...] = jnp.where(lane_o < 64, outs[0], outs[1]).astype(BF16)


def _pair_attn(q, k, v, B, S, tq, wq, shared_kv, lanes_a, lanes_b, name):
    T = q.shape[0]
    nq = S // tq
    kv_map = (lambda b, m, i: (b, 0)) if shared_kv else (lambda b, m, i: (b, m))
    return pl.pallas_call(
        functools.partial(_pair_kernel, lanes_a=lanes_a, lanes_b=lanes_b),
        grid=(B, 4, nq),
        in_specs=[pl.BlockSpec((tq, wq), lambda b, m, i: (b * nq + i, m)),
                  pl.BlockSpec((S, wq), kv_map),
                  pl.BlockSpec((S, 128), kv_map)],
        out_specs=pl.BlockSpec((tq, 128), lambda b, m, i: (b * nq + i, m)),
        out_shape=jax.ShapeDtypeStruct((T, 512), BF16),
        compiler_params=_cparams(),
        name=name,
    )(q, k, v)


def _diff_kernel(lam_ref, gsub_ref, q_ref, k_ref, v_ref, bias_ref, o_ref, s_scr, *, S, tq, lambda_init):
    i = pl.program_id(2)
    r = tq // QB
    q = q_ref[...]
    k = k_ref[...]
    lp = lam_ref[...]
    lam = (jnp.exp(jnp.sum(lp[0:1] * lp[1:2], axis=-1, keepdims=True))
           - jnp.exp(jnp.sum(lp[2:3] * lp[3:4], axis=-1, keepdims=True)) + lambda_init)
    lane = lax.broadcasted_iota(jnp.int32, (1, LANES), 1)
    probs = []
    for c in range(2):
        keep = (lane < HEAD_DIM) if c == 0 else (lane >= HEAD_DIM)
        qm = jnp.where(keep, q, jnp.zeros_like(q))
        s_scr[...] = lax.dot_general(qm, k, _NT, preferred_element_type=F32)
        for j in range(S // QB):
            u = jnp.clip(j - r * i + 2, 0, r + 3)
            s_scr[:, QB * j:QB * (j + 1)] = s_scr[:, QB * j:QB * (j + 1)] + bias_ref[0, u]
        e, l = _softmax_rows(s_scr[...])
        probs.append(e * (1.0 / l))
    a = (probs[0] - lam * probs[1]).astype(BF16)
    o = jnp.dot(a, v_ref[...], preferred_element_type=F32)
    o_ref[...] = (_rms(o, gsub_ref[...]) * (1.0 - lambda_init)).astype(BF16)


def _diff_attn(qd, kd, vd, lam_p, gsub, bias, B, S, tq, lambda_init):
    T = qd.shape[0]
    nq = S // tq
    nt = bias.shape[1]
    return pl.pallas_call(
        functools.partial(_diff_kernel, S=S, tq=tq, lambda_init=lambda_init),
        grid=(B, D_HEADS, nq),
        in_specs=[_const_spec(lam_p.shape), _const_spec(gsub.shape),
                  pl.BlockSpec((tq, 128), lambda b, h, i: (b * nq + i, h)),
                  pl.BlockSpec((S, 128), lambda b, h, i: (b, h)),
                  pl.BlockSpec((S, 128), lambda b, h, i: (b, h)),
                  pl.BlockSpec((1, nt, tq, 128), lambda b, h, i: (h, 0, 0, 0))],
        out_specs=pl.BlockSpec((tq, 128), lambda b, h, i: (b * nq + i, h)),
        out_shape=jax.ShapeDtypeStruct((T, 512), BF16),
        scratch_shapes=[pltpu.VMEM((tq, S), F32)],
        compiler_params=_cparams(),
        name="diff_attn",
    )(lam_p, gsub, qd, kd, vd, bias)


def _norm_matmul_kernel(x_ref, g_ref, w_ref, o_ref):
    hn = _rms(x_ref[...], g_ref[...]).astype(BF16)
    o_ref[...] = jnp.dot(hn, w_ref[...], preferred_element_type=F32).astype(o_ref.dtype)


def _norm_matmul(x, g, w, tm, out_dtype):
    T, K = x.shape
    N = w.shape[1]
    return pl.pallas_call(
        _norm_matmul_kernel,
        grid=(T // tm,),
        in_specs=[pl.BlockSpec((tm, K), lambda i: (i, 0)), _const_spec(g.shape), _const_spec(w.shape)],
        out_specs=pl.BlockSpec((tm, N), lambda i: (i, 0)),
        out_shape=jax.ShapeDtypeStruct((T, N), out_dtype),
        compiler_params=_cparams(),
        name="norm_matmul",
    )(x, g, w)


def _post_kernel(x_ref, oa_ref, ob_ref, wout_ref, gx_ref, wq_ref, kv_ref, wo_ref, gf_ref, wr_ref,
                 x2_ref, hn_ref, aff_ref):
    x1 = (x_ref[...]
          + jnp.dot(oa_ref[...], wout_ref[0:512, :], preferred_element_type=F32)
          + jnp.dot(ob_ref[...], wout_ref[512:1024, :], preferred_element_type=F32))
    hc = _rms(x1, gx_ref[...]).astype(BF16)
    q = (jnp.dot(hc, wq_ref[...], preferred_element_type=F32) * (X_HEAD_DIM ** -0.5)).astype(BF16)
    heads = []
    for h in range(X_HEADS):
        lo = X_HEAD_DIM * h
        kh = kv_ref[:, lo:lo + X_HEAD_DIM]
        vh = kv_ref[:, D_MODEL + lo:D_MODEL + lo + X_HEAD_DIM]
        s = lax.dot_general(q[:, lo:lo + X_HEAD_DIM], kh, _NT, preferred_element_type=F32)
        e, l = _softmax_rows(s)
        heads.append((jnp.dot(e.astype(BF16), vh, preferred_element_type=F32) / l).astype(BF16))
    o = jnp.concatenate(heads, axis=1)
    x2 = x1 + jnp.dot(o, wo_ref[...], preferred_element_type=F32)
    x2_ref[...] = x2
    hf = _rms(x2, gf_ref[...])
    hn_ref[...] = hf.astype(BF16)
    h1 = hf.astype(BF16)
    r1 = hf - h1.astype(F32)
    h2 = r1.astype(BF16)
    h3 = (r1 - h2.astype(F32)).astype(BF16)
    wr = wr_ref[...]
    logits = jnp.zeros((N_EXPERTS, hf.shape[0]), F32)
    for a, hpart in enumerate((h1, h2, h3)):
        for b in range(3 - a):
            logits = logits + lax.dot_general(wr[b], hpart, _NT, preferred_element_type=F32)
    mx = jnp.max(logits, axis=0, keepdims=True)
    e = jnp.exp(logits - mx)
    aff = e / jnp.sum(e, axis=0, keepdims=True)
    for c in range(aff.shape[1] // LANES):
        aff_ref[c] = aff[:, LANES * c:LANES * (c + 1)]


def _post(x, oa, ob, wout, gx, wq, kv, wo, gf, wr3, S, n_mem, tm):
    T = x.shape[0]
    per_b = S // tm
    row = lambda w: pl.BlockSpec((tm, w), lambda i: (i, 0))
    return pl.pallas_call(
        _post_kernel,
        grid=(T // tm,),
        in_specs=[row(D_MODEL), row(512), row(512), _const_spec(wout.shape), _const_spec(gx.shape),
                  _const_spec(wq.shape), pl.BlockSpec((n_mem, 2 * D_MODEL), lambda i: (i // per_b, 0)),
                  _const_spec(wo.shape), _const_spec(gf.shape), _const_spec(wr3.shape)],
        out_specs=[row(D_MODEL), row(D_MODEL),
                   pl.BlockSpec((tm // LANES, N_EXPERTS, LANES), lambda i: (i, 0, 0))],
        out_shape=[jax.ShapeDtypeStruct((T, D_MODEL), F32), jax.ShapeDtypeStruct((T, D_MODEL), BF16),
                   jax.ShapeDtypeStruct((T // LANES, N_EXPERTS, LANES), F32)],
        compiler_params=_cparams(),
        name="post",
    )(x, oa, ob, wout, gx, wq, kv, wo, gf, wr3)


def _route_kernel(aff_ref, tri_ref, ones_ref, carrymat_ref, key_ref, carry_ref, *, cap):
    x = aff_ref[...]
    nj = x.shape[0]

    def count(mask):
        c = jnp.sum(mask.astype(F32), axis=0, keepdims=True)
        return jnp.sum(c, axis=2, keepdims=True)

    def bisect(b, thr):
        cand = thr | lax.shift_left(jnp.int32(1), 30 - b)
        return jnp.where(count(x >= pltpu.bitcast(cand, F32)) >= cap, cand, thr)

    thr = pltpu.bitcast(lax.fori_loop(0, 31, bisect, jnp.zeros((1, N_EXPERTS, 1), jnp.int32)), F32)
    gt = x > thr
    eq = x == thr
    need = cap - count(gt)

    def prefix(mask):
        v = mask.astype(BF16).reshape(nj * N_EXPERTS, LANES)
        within = jnp.dot(v, tri_ref[...], preferred_element_type=F32)
        tot = jnp.dot(v, ones_ref[...], preferred_element_type=F32)
        carry = jnp.dot(carrymat_ref[...], tot.astype(BF16), preferred_element_type=F32)
        return within + carry, carry

    eq_rank, _ = prefix(eq)
    sel = gt | (eq & (eq_rank.reshape(nj, N_EXPERTS, LANES) < need))
    rank, carry = prefix(sel)
    key_ref[...] = jnp.where(sel, rank.reshape(nj, N_EXPERTS, LANES), -1.0)
    carry_ref[...] = carry


def _route(aff3, tri, ones, carrymat, cap):
    nj = aff3.shape[0]
    rows = nj * N_EXPERTS
    return pl.pallas_call(
        functools.partial(_route_kernel, cap=cap),
        out_shape=[jax.ShapeDtypeStruct((nj, N_EXPERTS, LANES), F32),
                   jax.ShapeDtypeStruct((rows, LANES), F32)],
        compiler_params=_cparams(),
        name="route",
    )(aff3, tri, ones, carrymat)


def _window_start(c0, cap, win):
    a = jnp.minimum((c0 // 16) * 16, cap - win)
    return pl.multiple_of(a, 16)


def _gather_kernel(c0_ref, key_ref, hn_ref, xe_ref, *, cap, win, nq):
    e = pl.program_id(0)
    sb = pl.program_id(1)

    @pl.when(sb == 0)
    def _():
        xe_ref[...] = jnp.zeros_like(xe_ref)

    slot = lax.broadcasted_iota(jnp.int32, (win, 1), 0).astype(F32)
    for q in range(nq):
        tb = sb * nq + q
        c0 = c0_ref[e, tb]
        c1 = c0_ref[e, tb + 1]
        a0 = _window_start(c0, cap, win)
        keyrow = jnp.concatenate([key_ref[0, 2 * q:2 * q + 1, :], key_ref[0, 2 * q + 1:2 * q + 2, :]], axis=1)

        def add_window(a, lo):
            hit = (keyrow - a.astype(F32) == slot) & (keyrow >= lo.astype(F32))
            rows = jnp.dot(hit.astype(BF16), hn_ref[256 * q:256 * (q + 1), :],
                           preferred_element_type=F32)
            xe_ref[0, pl.ds(a, win), :] = xe_ref[0, pl.ds(a, win), :] + rows.astype(BF16)

        pl.when(c1 > c0)(lambda: add_window(a0, c0))
        pl.when(c1 > a0 + win)(lambda: add_window(_window_start(a0 + win, cap, win), a0 + win))


def _gather(c0, key_em, hn, cap, win, tsb):
    T, D = hn.shape
    nq = tsb // 256
    grid_spec = pltpu.PrefetchScalarGridSpec(
        num_scalar_prefetch=1,
        grid=(N_EXPERTS, T // tsb),
        in_specs=[pl.BlockSpec((1, tsb // LANES, LANES), lambda e, sb, c0: (e, sb, 0)),
                  pl.BlockSpec((tsb, D), lambda e, sb, c0: (sb, 0))],
        out_specs=pl.BlockSpec((1, cap, D), lambda e, sb, c0: (e, 0, 0)),
    )
    return pl.pallas_call(
        functools.partial(_gather_kernel, cap=cap, win=win, nq=nq),
        grid_spec=grid_spec,
        out_shape=jax.ShapeDtypeStruct((N_EXPERTS, cap, D), BF16),
        compiler_params=_cparams(),
        name="moe_gather",
    )(c0, key_em, hn)


def _ffn_kernel(xe_ref, wg_ref, wu_ref, wd_ref, y_ref, acc_ref):
    f = pl.program_id(2)
    xe = xe_ref[0]
    a = jnp.dot(xe, wg_ref[0].astype(BF16), preferred_element_type=F32)
    u = jnp.dot(xe, wu_ref[0].astype(BF16), preferred_element_type=F32)
    h = (a * (1.0 / (1.0 + jnp.exp(-a))) * u).astype(BF16)
    part = jnp.dot(h, wd_ref[0].astype(BF16), preferred_element_type=F32)

    @pl.when(f == 0)
    def _():
        acc_ref[...] = part

    @pl.when(f > 0)
    def _():
        acc_ref[...] = acc_ref[...] + part

    @pl.when(f == pl.num_programs(2) - 1)
    def _():
        y_ref[0] = acc_ref[...].astype(BF16)


def _expert_ffn(xe, wg, wu, wd, tc, tf):
    E, cap, D = xe.shape
    F = wg.shape[2]
    return pl.pallas_call(
        _ffn_kernel,
        grid=(E, cap // tc, F // tf),
        in_specs=[pl.BlockSpec((1, tc, D), lambda e, c, f: (e, c, 0)),
                  pl.BlockSpec((1, D, tf), lambda e, c, f: (e, 0, f)),
                  pl.BlockSpec((1, D, tf), lambda e, c, f: (e, 0, f)),
                  pl.BlockSpec((1, tf, D), lambda e, c, f: (e, f, 0))],
        out_specs=pl.BlockSpec((1, tc, D), lambda e, c, f: (e, c, 0)),
        out_shape=jax.ShapeDtypeStruct((E, cap, D), BF16),
        scratch_shapes=[pltpu.VMEM((tc, D), F32)],
        compiler_params=_cparams(),
        name="expert_ffn",
    )(xe, wg, wu, wd)


def _combine_kernel(c0_ref, x_ref, key_ref, aff_ref, y_ref, o_ref, *, cap, win, nq):
    sb = pl.program_id(0)
    e = pl.program_id(1)

    @pl.when(e == 0)
    def _():
        o_ref[...] = x_ref[...]

    mine = lax.broadcasted_iota(jnp.int32, (1, N_EXPERTS), 1) == e
    keycol = jnp.sum(jnp.where(mine, key_ref[...], 0.0), axis=-1, keepdims=True)
    gate = jnp.sum(jnp.where(mine, aff_ref[...], 0.0), axis=-1, keepdims=True)
    slot = lax.broadcasted_iota(jnp.int32, (1, win), 1).astype(F32)
    for q in range(nq):
        tb = sb * nq + q
        c0 = c0_ref[e, tb]
        c1 = c0_ref[e, tb + 1]
        a0 = _window_start(c0, cap, win)
        kq = keycol[256 * q:256 * (q + 1)]
        gq = gate[256 * q:256 * (q + 1)]

        def add_window(a, lo):
            hit = (kq - a.astype(F32) == slot) & (kq >= lo.astype(F32))
            rows = jnp.dot(hit.astype(BF16), y_ref[0, pl.ds(a, win), :], preferred_element_type=F32)
            o_ref[256 * q:256 * (q + 1), :] = o_ref[256 * q:256 * (q + 1), :] + rows * gq

        pl.when(c1 > c0)(lambda: add_window(a0, c0))
        pl.when(c1 > a0 + win)(lambda: add_window(_window_start(a0 + win, cap, win), a0 + win))


def _combine(c0, x, key_tm, aff_tm, y, cap, win, tsb):
    T, D = x.shape
    nq = tsb // 256
    grid_spec = pltpu.PrefetchScalarGridSpec(
        num_scalar_prefetch=1,
        grid=(T // tsb, N_EXPERTS),
        in_specs=[pl.BlockSpec((tsb, D), lambda sb, e, c0: (sb, 0)),
                  pl.BlockSpec((tsb, N_EXPERTS), lambda sb, e, c0: (sb, 0)),
                  pl.BlockSpec((tsb, N_EXPERTS), lambda sb, e, c0: (sb, 0)),
                  pl.BlockSpec((1, cap, D), lambda sb, e, c0: (e, 0, 0))],
        out_specs=pl.BlockSpec((tsb, D), lambda sb, e, c0: (sb, 0)),
    )
    return pl.pallas_call(
        functools.partial(_combine_kernel, cap=cap, win=win, nq=nq),
        grid_spec=grid_spec,
        out_shape=jax.ShapeDtypeStruct((T, D), F32),
        compiler_params=_cparams(),
        name="moe_combine",
    )(c0, x, key_tm, aff_tm, y)


def _route_constants(nj):
    lane = jnp.arange(LANES)
    tri = (lane[:, None] < lane[None, :]).astype(BF16)
    ones = jnp.ones((LANES, LANES), BF16)
    r = jnp.arange(nj * N_EXPERTS)
    carrymat = ((r[None, :] % N_EXPERTS == r[:, None] % N_EXPERTS) & (r[None, :] < r[:, None])).astype(BF16)
    return tri, ones, carrymat


def _moe(x2, hn, aff3, wg, wu, wd, cap, tc, tf):
    T, D = x2.shape
    nj = T // LANES
    win = min(256, cap)
    tsb = min(2048, T)
    key3, carry = _route(aff3, *_route_constants(nj), cap)
    key_em = key3.transpose(1, 0, 2)
    key_tm = key3.transpose(0, 2, 1).reshape(T, N_EXPERTS)
    aff_tm = aff3.transpose(0, 2, 1).reshape(T, N_EXPERTS)
    starts = carry[:, 0].reshape(nj, N_EXPERTS)[::2].T.astype(jnp.int32)
    c0 = jnp.concatenate([starts, jnp.full((N_EXPERTS, 1), cap, jnp.int32)], axis=1)
    xe = _gather(c0, key_em, hn, cap, win, tsb)
    y = _expert_ffn(xe, wg, wu, wd, tc, tf)
    return _combine(c0, x2, key_tm, aff_tm, y, cap, win, tsb)


def _final_norm_kernel(x_ref, g_ref, o_ref):
    o_ref[...] = _rms(x_ref[...], g_ref[...])


def _final_norm(x, g, tm):
    T, D = x.shape
    return pl.pallas_call(
        _final_norm_kernel,
        grid=(T // tm,),
        in_specs=[pl.BlockSpec((tm, D), lambda i: (i, 0)), _const_spec(g.shape)],
        out_specs=pl.BlockSpec((tm, D), lambda i: (i, 0)),
        out_shape=jax.ShapeDtypeStruct((T, D), F32),
        name="final_norm",
    )(x, g)


def _t5_bucket(rel):
    half = N_BUCKETS // 2
    exact = half // 2
    n = jnp.abs(rel)
    big = exact + (jnp.log(jnp.maximum(n, 1).astype(F32) / exact)
                   / math.log(MAX_DISTANCE / exact) * (half - exact)).astype(jnp.int32)
    big = jnp.minimum(big, half - 1)
    return jnp.where(rel > 0, half, 0) + jnp.where(n < exact, n, big)


def _rope_freqs(pos, dim):
    freqs = ROPE_BASE ** (-jnp.arange(0, dim, 2, dtype=F32) / dim)
    ang = pos.astype(F32)[:, None] * freqs[None, :]
    return jnp.cos(ang), jnp.sin(ang)


def _rope_lane_table(cos, sin, active):
    w = cos.shape[1]
    first = (jnp.arange(w) % 32) < 16
    c = jnp.where(active[None, :], cos, 1.0)
    s1 = jnp.where((active & first)[None, :], -sin, 0.0)
    s2 = jnp.where((active & ~first)[None, :], sin, 0.0)
    return jnp.stack([c, s1, s2]).astype(F32)


def _head_pair_perm():
    cols = []
    for m in range(4):
        cols.append(jnp.arange(64 * m, 64 * (m + 1)))
        cols.append(jnp.arange(64 * (4 + m), 64 * (5 + m)))
    return jnp.concatenate(cols)


def _tables(S, tq_diff, rel_bias):
    t = jnp.arange(S)
    lane128 = jnp.arange(128)
    lane256 = jnp.arange(256)
    cos_t, sin_t = _rope_freqs(t, B_ROPE)
    scale_b = (B_NOPE + B_ROPE) ** -0.5
    tq = _rope_lane_table(cos_t[:, lane256 % 16], sin_t[:, lane256 % 16],
                          (lane256 >= 128) & (lane256 < 192)) * scale_b
    tk = _rope_lane_table(cos_t[:, lane128 % 16], sin_t[:, lane128 % 16], lane128 < 64)
    cos_r, sin_r = _rope_freqs(t // GRID_W, HEAD_DIM // 2)
    cos_c, sin_c = _rope_freqs(t % GRID_W, HEAD_DIM // 2)
    is_row = ((lane128 % 64) < 32)[None, :]
    cos_ax = jnp.where(is_row, cos_r[:, lane128 % 16], cos_c[:, lane128 % 16])
    sin_ax = jnp.where(is_row, sin_r[:, lane128 % 16], sin_c[:, lane128 % 16])
    tax = _rope_lane_table(cos_ax, sin_ax, jnp.ones((128,), bool))
    table = rel_bias.astype(F32)
    rel_win = jnp.arange(3 * QB)[None, :] - QB - jnp.arange(QB)[:, None]
    a_bias = jnp.transpose(table[_t5_bucket(rel_win)][..., :A_HEADS], (2, 0, 1))
    a_bias = jnp.where((jnp.abs(rel_win) <= WINDOW)[None], a_bias, NEG_INF)
    r = tq_diff // QB
    rel_d = (QB * (jnp.arange(r + 4)[:, None, None] - 2)
             + jnp.arange(QB)[None, None, :] - jnp.arange(tq_diff)[None, :, None])
    d_bias = jnp.transpose(table[:, A_HEADS:][_t5_bucket(rel_d)], (3, 0, 1, 2))
    return dict(tq=tq, tk=tk, tax=tax, a_bias=a_bias, d_bias=d_bias)


def _prep_weights(p):
    perm = _head_pair_perm()
    row = lambda v: v.reshape(1, -1).astype(F32)
    L = {}
    L['norm_mix'] = [row(p['norm_mix'][l]) for l in range(DEPTH)]
    L['norm_cross'] = [row(p['norm_cross'][l]) for l in range(DEPTH)]
    L['norm_mem'] = [row(p['norm_mem'][l]) for l in range(DEPTH)]
    L['norm_ffn'] = [row(p['norm_ffn'][l]) for l in range(DEPTH)]
    even, odd = [], []
    for i in range((DEPTH + 1) // 2):
        w = p['w_in_even'][i]
        kr = w[:, 1408:1440]
        win = jnp.concatenate([w[:, 0:512][:, perm], w[:, 512:1408], kr, kr,
                               jnp.zeros((D_MODEL, 64), w.dtype)], axis=1).astype(BF16)
        uq = p['b_w_uq'][i].reshape(B_Q_LORA, B_HEADS, B_NOPE + B_ROPE)
        blocks = []
        for m in range(4):
            blocks += [uq[:, 2 * m, :B_NOPE], uq[:, 2 * m + 1, :B_NOPE],
                       uq[:, 2 * m, B_NOPE:], uq[:, 2 * m + 1, B_NOPE:],
                       jnp.zeros((B_Q_LORA, 64), uq.dtype)]
        wuq = jnp.concatenate(blocks, axis=1).astype(BF16)
        ukv = p['b_w_ukv'][i].reshape(B_KV_LORA, B_HEADS, B_NOPE + B_V)
        wukv = jnp.concatenate([ukv[:, :, :B_NOPE].reshape(B_KV_LORA, -1),
                                ukv[:, :, B_NOPE:].reshape(B_KV_LORA, -1)], axis=1).astype(BF16)
        wo = p['w_out_even'][i]
        wout = jnp.concatenate([wo[0:512][perm], wo[512:1024]], axis=0).astype(BF16)
        sink = jnp.broadcast_to(p['a_sink'][i].astype(F32)[:, None], (A_HEADS, LANES))
        even.append(dict(win=win, wuq=wuq, wukv=wukv, wout=wout, sink=sink,
                         gq=row(p['b_q_norm'][i]), gkv=row(p['b_kv_norm'][i])))
    for i in range(DEPTH // 2):
        w = p['w_in_odd'][i]
        win = jnp.concatenate([w[:, 0:512][:, perm], w[:, 512:]], axis=1).astype(BF16)
        gqk = jnp.concatenate([jnp.tile(p['c_q_norm'][i], C_HEADS),
                               jnp.tile(p['c_k_norm'][i], C_KV_HEADS)]).reshape(1, -1).astype(F32)
        wo = p['w_out_odd'][i]
        wout = jnp.concatenate([wo[0:512][perm], wo[512:1024]], axis=0).astype(BF16)
        odd.append(dict(win=win, gqk=gqk, wout=wout, lam=p['d_lambda'][i].astype(F32),
                        gsub=row(p['d_subln'][i])))
    L['even'], L['odd'] = even, odd
    seg = jnp.arange(640) // HEAD_DIM
    L['ones'] = (seg[:, None] == seg[None, :]).astype(BF16)
    L['x_wq'] = [p['x_wq'][l].astype(BF16) for l in range(DEPTH)]
    L['x_wkv'] = [p['x_wkv'][l].astype(BF16) for l in range(DEPTH)]
    L['x_wo'] = [p['x_wo'][l].astype(BF16) for l in range(DEPTH)]
    wr3 = []
    for l in range(DEPTH):
        wt = p['router'][l].T.astype(F32)
        w1 = wt.astype(BF16)
        r1 = wt - w1.astype(F32)
        w2 = r1.astype(BF16)
        w3 = (r1 - w2.astype(F32)).astype(BF16)
        wr3.append(jnp.stack([w1, w2, w3]))
    L['router'] = wr3
    L['norm_final'] = row(p['norm_final'])
    return L


def _tile(n, pref):
    t = min(n, pref)
    assert n % t == 0, (n, t)
    return t


def _trunk(x, mem, p, L):
    B, S, D = x.shape
    n_mem = mem.shape[1]
    T = B * S
    tm_pre = _tile(S, 512)
    tm_post = _tile(S, 256)
    tq = _tile(S, 256)
    cap = EC_CAPACITY * T // N_EXPERTS
    tc = _tile(cap, 1024)
    tf = 512
    tabs = _tables(S, tq, p['rel_bias'])
    xf = x.reshape(T, D).astype(F32)
    memf = mem.reshape(B * n_mem, D).astype(F32)
    tm_mem = _tile(B * n_mem, 256)
    mla_a = ((0, 64), (128, 160))
    mla_b = ((64, 128), (160, 192))
    for layer in range(DEPTH):
        i = layer // 2
        if layer % 2 == 0:
            w = L['even'][i]
            qa, ka, va, qcat, kcat, vb = _pre_even(xf, S, L['norm_mix'][layer], w['win'], w['gq'], w['gkv'],
                                                   w['wuq'], w['wukv'], tabs['tq'], tabs['tk'], tm_pre)
            o1 = _window_attn(qa, ka, va, tabs['a_bias'], w['sink'], B, S)
            o2 = _pair_attn(qcat, kcat, vb, B, S, tq, 256, False, mla_a, mla_b, "latent_attn")
        else:
            w = L['odd'][i]
            qc, kc, vc, qd, kd, vd = _pre_odd(xf, S, L['norm_mix'][layer], w['win'], w['gqk'], L['ones'],
                                              tabs['tax'], tm_pre)
            o1 = _pair_attn(qc, kc, vc, B, S, tq, 128, True, ((0, 64),), ((64, 128),), "axial_attn")
            lambda_init = 0.8 - 0.6 * math.exp(-0.3 * layer)
            o2 = _diff_attn(qd, kd, vd, w['lam'], w['gsub'], tabs['d_bias'], B, S, tq, lambda_init)
        kv = _norm_matmul(memf, L['norm_mem'][layer], L['x_wkv'][layer], tm_mem, BF16)
        x2, hn, aff = _post(xf, o1, o2, w['wout'], L['norm_cross'][layer], L['x_wq'][layer], kv,
                            L['x_wo'][layer], L['norm_ffn'][layer], L['router'][layer], S, n_mem, tm_post)
        xf = _moe(x2, hn, aff, p['e_w_gate'][layer], p['e_w_up'][layer], p['e_w_down'][layer], cap, tc, tf)
    out = _final_norm(xf, L['norm_final'], tm_pre)
    return out.reshape(B, S, D)


def kernel(x_prompt, x_sample, mem_prompt, mem_sample, rel_bias, norm_mix, norm_cross, norm_mem, norm_ffn,
           norm_final, w_in_even, a_sink, b_q_norm, b_kv_norm, b_w_uq, b_w_ukv, w_out_even, w_in_odd,
           c_q_norm, c_k_norm, d_lambda, d_subln, w_out_odd, x_wq, x_wkv, x_wo, router, e_w_gate,
           e_w_up, e_w_down):
    p = dict(rel_bias=rel_bias, norm_mix=norm_mix, norm_cross=norm_cross, norm_mem=norm_mem,
             norm_ffn=norm_ffn, norm_final=norm_final, w_in_even=w_in_even, a_sink=a_sink,
             b_q_norm=b_q_norm, b_kv_norm=b_kv_norm, b_w_uq=b_w_uq, b_w_ukv=b_w_ukv,
             w_out_even=w_out_even, w_in_odd=w_in_odd, c_q_norm=c_q_norm, c_k_norm=c_k_norm,
             d_lambda=d_lambda, d_subln=d_subln, w_out_odd=w_out_odd, x_wq=x_wq, x_wkv=x_wkv,
             x_wo=x_wo, router=router, e_w_gate=e_w_gate, e_w_up=e_w_up, e_w_down=e_w_down)
    L = _prep_weights(p)
    y_prompt = _trunk(x_prompt, mem_prompt, p, L)
    y_sample = _trunk(x_sample, mem_sample, p, L)
    return (y_prompt, y_sample)
```

```python
import functools
import math

import jax
import jax.numpy as jnp
from jax import lax
from jax.experimental import pallas as pl
from jax.experimental.pallas import tpu as pltpu

F32 = jnp.float32
BF16 = jnp.bfloat16

D_MODEL = 1024
DEPTH = 4
HEAD_DIM = 64
QB = 128
WINDOW = 128
A_HEADS = 8
A_KV_HEADS = 2
B_HEADS = 8
B_Q_LORA = 384
B_KV_LORA = 256
B_NOPE = 64
B_ROPE = 32
B_V = 64
C_HEADS = 8
C_KV_HEADS = 2
D_HEADS = 4
X_HEADS = 4
X_HEAD_DIM = D_MODEL // X_HEADS
N_BUCKETS = 32
MAX_DISTANCE = 128
N_EXPERTS = 16
EXPERT_FF = 2 * D_MODEL
EC_CAPACITY = 2
GRID_W = 64
ROPE_BASE = 10000.0
NEG_INF = -1e30
EPS = 1e-6

LANES = 128
VMEM_LIMIT = 48 * 1024 * 1024

_NT = (((1,), (1,)), ((), ()))


def _cparams():
    return pltpu.CompilerParams(vmem_limit_bytes=VMEM_LIMIT)


def _rms(x, g):
    ms = jnp.mean(x * x, axis=-1, keepdims=True)
    return x * lax.rsqrt(ms + EPS) * g


def _rope(x, tab_ref):
    w = x.shape[1]
    return (x * tab_ref[0]
            + pltpu.roll(x, w - 16, 1) * tab_ref[1]
            + pltpu.roll(x, 16, 1) * tab_ref[2])


def _softmax_rows(s):
    m = jnp.max(s, axis=-1, keepdims=True)
    e = jnp.exp(s - m)
    return e, jnp.sum(e, axis=-1, keepdims=True)


def _pre_even_kernel(x_ref, g_ref, win_ref, gq_ref, gkv_ref, wuq_ref, wukv_ref, tq_ref, tk_ref,
                     qa_ref, ka_ref, va_ref, qcat_ref, kcat_ref, vb_ref):
    hn = _rms(x_ref[...], g_ref[...]).astype(BF16)
    z = jnp.dot(hn, win_ref[...], preferred_element_type=F32)
    qa_ref[...] = (z[:, 0:512] * (HEAD_DIM ** -0.5)).astype(BF16)
    ka_ref[...] = z[:, 512:640].astype(BF16)
    va_ref[...] = z[:, 640:768].astype(BF16)
    cq = _rms(z[:, 768:1152], gq_ref[...]).astype(BF16)
    q = jnp.dot(cq, wuq_ref[...], preferred_element_type=F32)
    for m in range(4):
        qcat_ref[:, 256 * m:256 * (m + 1)] = _rope(q[:, 256 * m:256 * (m + 1)], tq_ref).astype(BF16)
    ckv = _rms(z[:, 1152:1408], gkv_ref[...]).astype(BF16)
    kv = jnp.dot(ckv, wukv_ref[...], preferred_element_type=F32)
    kr = _rope(z[:, 1408:1536], tk_ref).astype(BF16)
    for m in range(4):
        kcat_ref[:, 256 * m:256 * m + 128] = kv[:, 128 * m:128 * (m + 1)].astype(BF16)
        kcat_ref[:, 256 * m + 128:256 * (m + 1)] = kr
    vb_ref[...] = kv[:, 512:1024].astype(BF16)


def _pre_odd_kernel(x_ref, g_ref, win_ref, gqk_ref, ones_ref, tab_ref,
                    qc_ref, kc_ref, vc_ref, qd_ref, kd_ref, vd_ref):
    hn = _rms(x_ref[...], g_ref[...]).astype(BF16)
    z = jnp.dot(hn, win_ref[...], preferred_element_type=F32)
    qk = z[:, 0:640]
    ss = qk * qk
    hi = ss.astype(BF16)
    lo = (ss - hi.astype(F32)).astype(BF16)
    ms = (jnp.dot(hi, ones_ref[...], preferred_element_type=F32)
          + jnp.dot(lo, ones_ref[...], preferred_element_type=F32)) * (1.0 / HEAD_DIM)
    qkn = qk * lax.rsqrt(ms + EPS) * gqk_ref[...]
    for m in range(4):
        blk = _rope(qkn[:, 128 * m:128 * (m + 1)], tab_ref)
        qc_ref[:, 128 * m:128 * (m + 1)] = (blk * (HEAD_DIM ** -0.5)).astype(BF16)
    kc_ref[...] = _rope(qkn[:, 512:640], tab_ref).astype(BF16)
    vc_ref[...] = z[:, 640:768].astype(BF16)
    qd_ref[...] = (z[:, 768:1280] * (HEAD_DIM ** -0.5)).astype(BF16)
    kd_ref[...] = z[:, 1280:1792].astype(BF16)
    vd_ref[...] = z[:, 1792:2304].astype(BF16)


def _const_spec(shape):
    nd = len(shape)
    return pl.BlockSpec(shape, lambda *_: (0,) * nd)


def _pre_even(x, S, g, win, gq, gkv, wuq, wukv, tq, tk, tm):
    T = x.shape[0]
    ns = S // tm
    row = lambda w: pl.BlockSpec((tm, w), lambda i: (i, 0))
    tab = lambda w: pl.BlockSpec((3, tm, w), lambda i: (0, i % ns, 0))
    out_w = (512, 128, 128, 1024, 1024, 512)
    return pl.pallas_call(
        _pre_even_kernel,
        grid=(T // tm,),
        in_specs=[row(D_MODEL), _const_spec(g.shape), _const_spec(win.shape), _const_spec(gq.shape),
                  _const_spec(gkv.shape), _const_spec(wuq.shape), _const_spec(wukv.shape),
                  tab(256), tab(128)],
        out_specs=[row(w) for w in out_w],
        out_shape=[jax.ShapeDtypeStruct((T, w), BF16) for w in out_w],
        compiler_params=_cparams(),
        name="pre_even",
    )(x, g, win, gq, gkv, wuq, wukv, tq, tk)


def _pre_odd(x, S, g, win, gqk, ones, tab3, tm):
    T = x.shape[0]
    ns = S // tm
    row = lambda w: pl.BlockSpec((tm, w), lambda i: (i, 0))
    out_w = (512, 128, 128, 512, 512, 512)
    return pl.pallas_call(
        _pre_odd_kernel,
        grid=(T // tm,),
        in_specs=[row(D_MODEL), _const_spec(g.shape), _const_spec(win.shape), _const_spec(gqk.shape),
                  _const_spec(ones.shape), pl.BlockSpec((3, tm, 128), lambda i: (0, i % ns, 0))],
        out_specs=[row(w) for w in out_w],
        out_shape=[jax.ShapeDtypeStruct((T, w), BF16) for w in out_w],
        compiler_params=_cparams(),
        name="pre_odd",
    )(x, g, win, gqk, ones, tab3)


def _window_kernel(q_ref, kp_ref, kc_ref, kn_ref, vp_ref, vc_ref, vn_ref, bias_ref, sink_ref, o_ref):
    i = pl.program_id(1)
    nb = pl.num_programs(1)
    q = q_ref[...]
    kw = jnp.concatenate([kp_ref[...], kc_ref[...], kn_ref[...]], axis=0)
    vw = jnp.concatenate([vp_ref[...], vc_ref[...], vn_ref[...]], axis=0)
    col = lax.broadcasted_iota(jnp.int32, (1, 3 * QB), 1)
    outside = ((col < QB) & (i == 0)) | ((col >= 2 * QB) & (i == nb - 1))
    lane = lax.broadcasted_iota(jnp.int32, (1, LANES), 1)
    for m in range(4):
        qblk = q[:, 128 * m:128 * (m + 1)]
        outs = []
        for half in range(2):
            h = m + 4 * half
            keep = (lane < HEAD_DIM) if half == 0 else (lane >= HEAD_DIM)
            qm = jnp.where(keep, qblk, jnp.zeros_like(qblk))
            s = lax.dot_general(qm, kw, _NT, preferred_element_type=F32) + bias_ref[h]
            s = jnp.where(outside, NEG_INF, s)
            sk = sink_ref[h:h + 1, 0:1]
            mx = jnp.maximum(jnp.max(s, axis=-1, keepdims=True), sk)
            e = jnp.exp(s - mx)
            den = jnp.sum(e, axis=-1, keepdims=True) + jnp.exp(sk - mx)
            outs.append(jnp.dot(e.astype(BF16), vw, preferred_element_type=F32) / den)
        o_ref[:, 128 * m:128 * (m + 1)] = jnp.where(lane < HEAD_DIM, outs[0], outs[1]).astype(BF16)


def _window_attn(qa, ka, va, bias, sink, B, S):
    T = qa.shape[0]
    nb = S // QB
    kv_spec = lambda d: pl.BlockSpec(
        (QB, 128), lambda b, i: (b * nb + jnp.clip(i + d, 0, nb - 1), 0))
    return pl.pallas_call(
        _window_kernel,
        grid=(B, nb),
        in_specs=[pl.BlockSpec((QB, 512), lambda b, i: (b * nb + i, 0)),
                  kv_spec(-1), kv_spec(0), kv_spec(1), kv_spec(-1), kv_spec(0), kv_spec(1),
                  _const_spec(bias.shape), _const_spec(sink.shape)],
        out_specs=pl.BlockSpec((QB, 512), lambda b, i: (b * nb + i, 0)),
        out_shape=jax.ShapeDtypeStruct((T, 512), BF16),
        compiler_params=_cparams(),
        name="window_attn",
    )(qa, ka, ka, ka, va, va, va, bias, sink)


def _pair_kernel(q_ref, k_ref, v_ref, o_ref, *, lanes_a, lanes_b):
    q = q_ref[...]
    k = k_ref[...]
    v = v_ref[...]
    wq = q.shape[1]
    lane = lax.broadcasted_iota(jnp.int32, (1, wq), 1)
    outs = []
    for ranges in (lanes_a, lanes_b):
        keep = functools.reduce(jnp.logical_or, [(lane >= lo) & (lane < hi) for lo, hi in ranges])
        qm = jnp.where(keep, q, jnp.zeros_like(q))
        s = lax.dot_general(qm, k, _NT, preferred_element_type=F32)
        e, l = _softmax_rows(s)
        outs.append(jnp.dot(e.astype(BF16), v, preferred_element_type=F32) / l)
    lane_o = lax.broadcasted_iota(jnp.int32, (1, LANES), 1)
    o_ref[...] = jnp.where(lane_o < 64, outs[0], outs[1]).astype(BF16)


def _pair_attn(q, k, v, B, S, tq, wq, shared_kv, lanes_a, lanes_b, name):
    T = q.shape[0]
    nq = S // tq
    kv_map = (lambda b, m, i: (b, 0)) if shared_kv else (lambda b, m, i: (b, m))
    return pl.pallas_call(
        functools.partial(_pair_kernel, lanes_a=lanes_a, lanes_b=lanes_b),
        grid=(B, 4, nq),
        in_specs=[pl.BlockSpec((tq, wq), lambda b, m, i: (b * nq + i, m)),
                  pl.BlockSpec((S, wq), kv_map),
                  pl.BlockSpec((S, 128), kv_map)],
        out_specs=pl.BlockSpec((tq, 128), lambda b, m, i: (b * nq + i, m)),
        out_shape=jax.ShapeDtypeStruct((T, 512), BF16),
        compiler_params=_cparams(),
        name=name,
    )(q, k, v)


def _diff_kernel(lam_ref, gsub_ref, q_ref, k_ref, v_ref, bias_ref, o_ref, s_scr, *, S, tq, lambda_init):
    i = pl.program_id(2)
    r = tq // QB
    q = q_ref[...]
    k = k_ref[...]
    lp = lam_ref[...]
    lam = (jnp.exp(jnp.sum(lp[0:1] * lp[1:2], axis=-1, keepdims=True))
           - jnp.exp(jnp.sum(lp[2:3] * lp[3:4], axis=-1, keepdims=True)) + lambda_init)
    lane = lax.broadcasted_iota(jnp.int32, (1, LANES), 1)
    probs = []
    for c in range(2):
        keep = (lane < HEAD_DIM) if c == 0 else (lane >= HEAD_DIM)
        qm = jnp.where(keep, q, jnp.zeros_like(q))
        s_scr[...] = lax.dot_general(qm, k, _NT, preferred_element_type=F32)
        for j in range(S // QB):
            u = jnp.clip(j - r * i + 2, 0, r + 3)
            s_scr[:, QB * j:QB * (j + 1)] = s_scr[:, QB * j:QB * (j + 1)] + bias_ref[0, u]
        e, l = _softmax_rows(s_scr[...])
        probs.append(e * (1.0 / l))
    a = (probs[0] - lam * probs[1]).astype(BF16)
    o = jnp.dot(a, v_ref[...], preferred_element_type=F32)
    o_ref[...] = (_rms(o, gsub_ref[...]) * (1.0 - lambda_init)).astype(BF16)


def _diff_attn(qd, kd, vd, lam_p, gsub, bias, B, S, tq, lambda_init):
    T = qd.shape[0]
    nq = S // tq
    nt = bias.shape[1]
    return pl.pallas_call(
        functools.partial(_diff_kernel, S=S, tq=tq, lambda_init=lambda_init),
        grid=(B, D_HEADS, nq),
        in_specs=[_const_spec(lam_p.shape), _const_spec(gsub.shape),
                  pl.BlockSpec((tq, 128), lambda b, h, i: (b * nq + i, h)),
                  pl.BlockSpec((S, 128), lambda b, h, i: (b, h)),
                  pl.BlockSpec((S, 128), lambda b, h, i: (b, h)),
                  pl.BlockSpec((1, nt, tq, 128), lambda b, h, i: (h, 0, 0, 0))],
        out_specs=pl.BlockSpec((tq, 128), lambda b, h, i: (b * nq + i, h)),
        out_shape=jax.ShapeDtypeStruct((T, 512), BF16),
        scratch_shapes=[pltpu.VMEM((tq, S), F32)],
        compiler_params=_cparams(),
        name="diff_attn",
    )(lam_p, gsub, qd, kd, vd, bias)


def _norm_matmul_kernel(x_ref, g_ref, w_ref, o_ref):
    hn = _rms(x_ref[...], g_ref[...]).astype(BF16)
    o_ref[...] = jnp.dot(hn, w_ref[...], preferred_element_type=F32).astype(o_ref.dtype)


def _norm_matmul(x, g, w, tm, out_dtype):
    T, K = x.shape
    N = w.shape[1]
    return pl.pallas_call(
        _norm_matmul_kernel,
        grid=(T // tm,),
        in_specs=[pl.BlockSpec((tm, K), lambda i: (i, 0)), _const_spec(g.shape), _const_spec(w.shape)],
        out_specs=pl.BlockSpec((tm, N), lambda i: (i, 0)),
        out_shape=jax.ShapeDtypeStruct((T, N), out_dtype),
        compiler_params=_cparams(),
        name="norm_matmul",
    )(x, g, w)


def _post_kernel(x_ref, oa_ref, ob_ref, wout_ref, gx_ref, wq_ref, kv_ref, wo_ref, gf_ref, wr_ref,
                 x2_ref, hn_ref, aff_ref):
    x1 = (x_ref[...]
          + jnp.dot(oa_ref[...], wout_ref[0:512, :], preferred_element_type=F32)
          + jnp.dot(ob_ref[...], wout_ref[512:1024, :], preferred_element_type=F32))
    hc = _rms(x1, gx_ref[...]).astype(BF16)
    q = (jnp.dot(hc, wq_ref[...], preferred_element_type=F32) * (X_HEAD_DIM ** -0.5)).astype(BF16)
    heads = []
    for h in range(X_HEADS):
        lo = X_HEAD_DIM * h
        kh = kv_ref[:, lo:lo + X_HEAD_DIM]
        vh = kv_ref[:, D_MODEL + lo:D_MODEL + lo + X_HEAD_DIM]
        s = lax.dot_general(q[:, lo:lo + X_HEAD_DIM], kh, _NT, preferred_element_type=F32)
        e, l = _softmax_rows(s)
        heads.append((jnp.dot(e.astype(BF16), vh, preferred_element_type=F32) / l).astype(BF16))
    o = jnp.concatenate(heads, axis=1)
    x2 = x1 + jnp.dot(o, wo_ref[...], preferred_element_type=F32)
    x2_ref[...] = x2
    hf = _rms(x2, gf_ref[...])
    hn_ref[...] = hf.astype(BF16)
    h1 = hf.astype(BF16)
    r1 = hf - h1.astype(F32)
    h2 = r1.astype(BF16)
    h3 = (r1 - h2.astype(F32)).astype(BF16)
    wr = wr_ref[...]
    logits = jnp.zeros((N_EXPERTS, hf.shape[0]), F32)
    for a, hpart in enumerate((h1, h2, h3)):
        for b in range(3 - a):
            logits = logits + lax.dot_general(wr[b], hpart, _NT, preferred_element_type=F32)
    mx = jnp.max(logits, axis=0, keepdims=True)
    e = jnp.exp(logits - mx)
    aff = e / jnp.sum(e, axis=0, keepdims=True)
    for c in range(aff.shape[1] // LANES):
        aff_ref[c] = aff[:, LANES * c:LANES * (c + 1)]


def _post(x, oa, ob, wout, gx, wq, kv, wo, gf, wr3, S, n_mem, tm):
    T = x.shape[0]
    per_b = S // tm
    row = lambda w: pl.BlockSpec((tm, w), lambda i: (i, 0))
    return pl.pallas_call(
        _post_kernel,
        grid=(T // tm,),
        in_specs=[row(D_MODEL), row(512), row(512), _const_spec(wout.shape), _const_spec(gx.shape),
                  _const_spec(wq.shape), pl.BlockSpec((n_mem, 2 * D_MODEL), lambda i: (i // per_b, 0)),
                  _const_spec(wo.shape), _const_spec(gf.shape), _const_spec(wr3.shape)],
        out_specs=[row(D_MODEL), row(D_MODEL),
                   pl.BlockSpec((tm // LANES, N_EXPERTS, LANES), lambda i: (i, 0, 0))],
        out_shape=[jax.ShapeDtypeStruct((T, D_MODEL), F32), jax.ShapeDtypeStruct((T, D_MODEL), BF16),
                   jax.ShapeDtypeStruct((T // LANES, N_EXPERTS, LANES), F32)],
        compiler_params=_cparams(),
        name="post",
    )(x, oa, ob, wout, gx, wq, kv, wo, gf, wr3)


def _route_kernel(aff_ref, tri_ref, ones_ref, carrymat_ref, key_ref, carry_ref, *, cap):
    x = aff_ref[...]
    nj = x.shape[0]

    def count(mask):
        c = jnp.sum(mask.astype(F32), axis=0, keepdims=True)
        return jnp.sum(c, axis=2, keepdims=True)

    def bisect(b, thr):
        cand = thr | lax.shift_left(jnp.int32(1), 30 - b)
        return jnp.where(count(x >= pltpu.bitcast(cand, F32)) >= cap, cand, thr)

    thr = pltpu.bitcast(lax.fori_loop(0, 31, bisect, jnp.zeros((1, N_EXPERTS, 1), jnp.int32)), F32)
    gt = x > thr
    eq = x == thr
    need = cap - count(gt)

    def prefix(mask):
        v = mask.astype(BF16).reshape(nj * N_EXPERTS, LANES)
        within = jnp.dot(v, tri_ref[...], preferred_element_type=F32)
        tot = jnp.dot(v, ones_ref[...], preferred_element_type=F32)
        carry = jnp.dot(carrymat_ref[...], tot.astype(BF16), preferred_element_type=F32)
        return within + carry, carry

    eq_rank, _ = prefix(eq)
    sel = gt | (eq & (eq_rank.reshape(nj, N_EXPERTS, LANES) < need))
    rank, carry = prefix(sel)
    key_ref[...] = jnp.where(sel, rank.reshape(nj, N_EXPERTS, LANES), -1.0)
    carry_ref[...] = carry


def _route(aff3, tri, ones, carrymat, cap):
    nj = aff3.shape[0]
    rows = nj * N_EXPERTS
    return pl.pallas_call(
        functools.partial(_route_kernel, cap=cap),
        out_shape=[jax.ShapeDtypeStruct((nj, N_EXPERTS, LANES), F32),
                   jax.ShapeDtypeStruct((rows, LANES), F32)],
        compiler_params=_cparams(),
        name="route",
    )(aff3, tri, ones, carrymat)


def _window_start(c0, cap, win):
    a = jnp.minimum((c0 // 16) * 16, cap - win)
    return pl.multiple_of(a, 16)


def _gather_kernel(c0_ref, key_ref, hn_ref, xe_ref, *, cap, win, nq):
    e = pl.program_id(0)
    sb = pl.program_id(1)

    @pl.when(sb == 0)
    def _():
        xe_ref[...] = jnp.zeros_like(xe_ref)

    slot = lax.broadcasted_iota(jnp.int32, (win, 1), 0).astype(F32)

    def add_window(q, a, lo):
        keyrow = jnp.concatenate([key_ref[0, 2 * q:2 * q + 1, :], key_ref[0, 2 * q + 1:2 * q + 2, :]], axis=1)
        hit = (keyrow - a.astype(F32) == slot) & (keyrow >= lo.astype(F32))
        rows = jnp.dot(hit.astype(BF16), hn_ref[256 * q:256 * (q + 1), :],
                       preferred_element_type=F32)
        xe_ref[0, pl.ds(a, win), :] = xe_ref[0, pl.ds(a, win), :] + rows.astype(BF16)

    starts = [_window_start(c0_ref[e, sb * nq + q], cap, win) for q in range(nq)]
    for q in range(nq):
        add_window(q, starts[q], c0_ref[e, sb * nq + q])
    for q in range(nq):
        nxt = starts[q] + win
        pl.when(c0_ref[e, sb * nq + q + 1] > nxt)(
            functools.partial(add_window, q, _window_start(nxt, cap, win), nxt))


def _gather(c0, key_em, hn, cap, win, tsb):
    T, D = hn.shape
    nq = tsb // 256
    grid_spec = pltpu.PrefetchScalarGridSpec(
        num_scalar_prefetch=1,
        grid=(N_EXPERTS, T // tsb),
        in_specs=[pl.BlockSpec((1, tsb // LANES, LANES), lambda e, sb, c0: (e, sb, 0)),
                  pl.BlockSpec((tsb, D), lambda e, sb, c0: (sb, 0))],
        out_specs=pl.BlockSpec((1, cap, D), lambda e, sb, c0: (e, 0, 0)),
    )
    return pl.pallas_call(
        functools.partial(_gather_kernel, cap=cap, win=win, nq=nq),
        grid_spec=grid_spec,
        out_shape=jax.ShapeDtypeStruct((N_EXPERTS, cap, D), BF16),
        compiler_params=_cparams(),
        name="moe_gather",
    )(c0, key_em, hn)


def _ffn_kernel(xe_ref, wg_ref, wu_ref, wd_ref, y_ref, acc_ref):
    f = pl.program_id(2)
    xe = xe_ref[0]
    a = jnp.dot(xe, wg_ref[0, 0].astype(BF16), preferred_element_type=F32)
    u = jnp.dot(xe, wu_ref[0, 0].astype(BF16), preferred_element_type=F32)
    h = (a * (1.0 / (1.0 + jnp.exp(-a))) * u).astype(BF16)
    part = jnp.dot(h, wd_ref[0, 0].astype(BF16), preferred_element_type=F32)

    @pl.when(f == 0)
    def _():
        acc_ref[...] = part

    @pl.when(f > 0)
    def _():
        acc_ref[...] = acc_ref[...] + part

    @pl.when(f == pl.num_programs(2) - 1)
    def _():
        y_ref[0] = acc_ref[...].astype(BF16)


def _expert_ffn(xe, wg, wu, wd, layer, tc, tf):
    E, cap, D = xe.shape
    F = wg.shape[3]
    return pl.pallas_call(
        _ffn_kernel,
        grid=(E, cap // tc, F // tf),
        in_specs=[pl.BlockSpec((1, tc, D), lambda e, c, f: (e, c, 0)),
                  pl.BlockSpec((1, 1, D, tf), lambda e, c, f: (layer, e, 0, f)),
                  pl.BlockSpec((1, 1, D, tf), lambda e, c, f: (layer, e, 0, f)),
                  pl.BlockSpec((1, 1, tf, D), lambda e, c, f: (layer, e, f, 0))],
        out_specs=pl.BlockSpec((1, tc, D), lambda e, c, f: (e, c, 0)),
        out_shape=jax.ShapeDtypeStruct((E, cap, D), BF16),
        scratch_shapes=[pltpu.VMEM((tc, D), F32)],
        compiler_params=_cparams(),
        name="expert_ffn",
    )(xe, wg, wu, wd)


def _combine_kernel(c0_ref, x_ref, key_ref, aff_ref, y_ref, o_ref, *, cap, win, nq):
    sb = pl.program_id(0)
    e = pl.program_id(1)

    @pl.when(e == 0)
    def _():
        o_ref[...] = x_ref[...]

    mine = lax.broadcasted_iota(jnp.int32, (1, N_EXPERTS), 1) == e
    keycol = jnp.sum(jnp.where(mine, key_ref[...], 0.0), axis=-1, keepdims=True)
    gate = jnp.sum(jnp.where(mine, aff_ref[...], 0.0), axis=-1, keepdims=True)
    slot = lax.broadcasted_iota(jnp.int32, (1, win), 1).astype(F32)

    def add_window(q, a, lo):
        kq = keycol[256 * q:256 * (q + 1)]
        hit = (kq - a.astype(F32) == slot) & (kq >= lo.astype(F32))
        rows = jnp.dot(hit.astype(BF16), y_ref[0, pl.ds(a, win), :], preferred_element_type=F32)
        o_ref[256 * q:256 * (q + 1), :] = (o_ref[256 * q:256 * (q + 1), :]
                                           + rows * gate[256 * q:256 * (q + 1)])

    starts = [_window_start(c0_ref[e, sb * nq + q], cap, win) for q in range(nq)]
    for q in range(nq):
        add_window(q, starts[q], c0_ref[e, sb * nq + q])
    for q in range(nq):
        nxt = starts[q] + win
        pl.when(c0_ref[e, sb * nq + q + 1] > nxt)(
            functools.partial(add_window, q, _window_start(nxt, cap, win), nxt))


def _combine(c0, x, key_tm, aff_tm, y, cap, win, tsb):
    T, D = x.shape
    nq = tsb // 256
    grid_spec = pltpu.PrefetchScalarGridSpec(
        num_scalar_prefetch=1,
        grid=(T // tsb, N_EXPERTS),
        in_specs=[pl.BlockSpec((tsb, D), lambda sb, e, c0: (sb, 0)),
                  pl.BlockSpec((tsb, N_EXPERTS), lambda sb, e, c0: (sb, 0)),
                  pl.BlockSpec((tsb, N_EXPERTS), lambda sb, e, c0: (sb, 0)),
                  pl.BlockSpec((1, cap, D), lambda sb, e, c0: (e, 0, 0))],
        out_specs=pl.BlockSpec((tsb, D), lambda sb, e, c0: (sb, 0)),
    )
    return pl.pallas_call(
        functools.partial(_combine_kernel, cap=cap, win=win, nq=nq),
        grid_spec=grid_spec,
        out_shape=jax.ShapeDtypeStruct((T, D), F32),
        compiler_params=_cparams(),
        name="moe_combine",
    )(c0, x, key_tm, aff_tm, y)


def _route_constants(nj):
    lane = jnp.arange(LANES)
    tri = (lane[:, None] < lane[None, :]).astype(BF16)
    ones = jnp.ones((LANES, LANES), BF16)
    r = jnp.arange(nj * N_EXPERTS)
    carrymat = ((r[None, :] % N_EXPERTS == r[:, None] % N_EXPERTS) & (r[None, :] < r[:, None])).astype(BF16)
    return tri, ones, carrymat


def _moe(x2, hn, aff3, wg, wu, wd, layer, cap, tc, tf):
    T, D = x2.shape
    nj = T // LANES
    win = min(256, cap)
    tsb = min(2048, T)
    key3, carry = _route(aff3, *_route_constants(nj), cap)
    key_em = key3.transpose(1, 0, 2)
    key_tm = key3.transpose(0, 2, 1).reshape(T, N_EXPERTS)
    aff_tm = aff3.transpose(0, 2, 1).reshape(T, N_EXPERTS)
    starts = carry[:, 0].reshape(nj, N_EXPERTS)[::2].T.astype(jnp.int32)
    c0 = jnp.concatenate([starts, jnp.full((N_EXPERTS, 1), cap, jnp.int32)], axis=1)
    xe = _gather(c0, key_em, hn, cap, win, tsb)
    y = _expert_ffn(xe, wg, wu, wd, layer, tc, tf)
    return _combine(c0, x2, key_tm, aff_tm, y, cap, win, tsb)


def _final_norm_kernel(x_ref, g_ref, o_ref):
    o_ref[...] = _rms(x_ref[...], g_ref[...])


def _final_norm(x, g, tm):
    T, D = x.shape
    return pl.pallas_call(
        _final_norm_kernel,
        grid=(T // tm,),
        in_specs=[pl.BlockSpec((tm, D), lambda i: (i, 0)), _const_spec(g.shape)],
        out_specs=pl.BlockSpec((tm, D), lambda i: (i, 0)),
        out_shape=jax.ShapeDtypeStruct((T, D), F32),
        name="final_norm",
    )(x, g)


def _t5_bucket(rel):
    half = N_BUCKETS // 2
    exact = half // 2
    n = jnp.abs(rel)
    big = exact + (jnp.log(jnp.maximum(n, 1).astype(F32) / exact)
                   / math.log(MAX_DISTANCE / exact) * (half - exact)).astype(jnp.int32)
    big = jnp.minimum(big, half - 1)
    return jnp.where(rel > 0, half, 0) + jnp.where(n < exact, n, big)


def _rope_freqs(pos, dim):
    freqs = ROPE_BASE ** (-jnp.arange(0, dim, 2, dtype=F32) / dim)
    ang = pos.astype(F32)[:, None] * freqs[None, :]
    return jnp.cos(ang), jnp.sin(ang)


def _rope_lane_table(cos, sin, active):
    w = cos.shape[1]
    first = (jnp.arange(w) % 32) < 16
    c = jnp.where(active[None, :], cos, 1.0)
    s1 = jnp.where((active & first)[None, :], -sin, 0.0)
    s2 = jnp.where((active & ~first)[None, :], sin, 0.0)
    return jnp.stack([c, s1, s2]).astype(F32)


def _pair_heads_cols(w):
    k = w.shape[0]
    return w.reshape(k, 2, 4, HEAD_DIM).transpose(0, 2, 1, 3).reshape(k, 8 * HEAD_DIM)


def _pair_heads_rows(w):
    n = w.shape[1]
    return w.reshape(2, 4, HEAD_DIM, n).transpose(1, 0, 2, 3).reshape(8 * HEAD_DIM, n)


def _bias_lookup(table, bucket):
    shape = (table.shape[1],) + (1,) * bucket.ndim
    out = jnp.zeros((table.shape[1],) + bucket.shape, F32)
    for b in range(N_BUCKETS):
        out = jnp.where(bucket[None] == b, table[b].reshape(shape), out)
    return out


def _tables(S, tq_diff, rel_bias):
    t = jnp.arange(S)
    lane128 = jnp.arange(128)
    lane256 = jnp.arange(256)
    cos_t, sin_t = _rope_freqs(t, B_ROPE)
    scale_b = (B_NOPE + B_ROPE) ** -0.5
    tq = _rope_lane_table(jnp.tile(cos_t, (1, 16)), jnp.tile(sin_t, (1, 16)),
                          (lane256 >= 128) & (lane256 < 192)) * scale_b
    tk = _rope_lane_table(jnp.tile(cos_t, (1, 8)), jnp.tile(sin_t, (1, 8)), lane128 < 64)
    cos_r, sin_r = _rope_freqs(t // GRID_W, HEAD_DIM // 2)
    cos_c, sin_c = _rope_freqs(t % GRID_W, HEAD_DIM // 2)
    cos_ax = jnp.tile(jnp.concatenate([cos_r, cos_r, cos_c, cos_c], axis=1), (1, 2))
    sin_ax = jnp.tile(jnp.concatenate([sin_r, sin_r, sin_c, sin_c], axis=1), (1, 2))
    tax = _rope_lane_table(cos_ax, sin_ax, jnp.ones((128,), bool))
    table = rel_bias.astype(F32)
    rel_win = jnp.arange(3 * QB)[None, :] - QB - jnp.arange(QB)[:, None]
    a_bias = _bias_lookup(table[:, :A_HEADS], _t5_bucket(rel_win))
    a_bias = jnp.where((jnp.abs(rel_win) <= WINDOW)[None], a_bias, NEG_INF)
    r = tq_diff // QB
    rel_d = (QB * (jnp.arange(r + 4)[:, None, None] - 2)
             + jnp.arange(QB)[None, None, :] - jnp.arange(tq_diff)[None, :, None])
    d_bias = _bias_lookup(table[:, A_HEADS:], _t5_bucket(rel_d))
    return dict(tq=tq, tk=tk, tax=tax, a_bias=a_bias, d_bias=d_bias)


def _prep_weights(p):
    row = lambda v: v.reshape(1, -1).astype(F32)
    L = {}
    L['norm_mix'] = [row(p['norm_mix'][l]) for l in range(DEPTH)]
    L['norm_cross'] = [row(p['norm_cross'][l]) for l in range(DEPTH)]
    L['norm_mem'] = [row(p['norm_mem'][l]) for l in range(DEPTH)]
    L['norm_ffn'] = [row(p['norm_ffn'][l]) for l in range(DEPTH)]
    even, odd = [], []
    for i in range((DEPTH + 1) // 2):
        w = p['w_in_even'][i]
        kr = w[:, 1408:1440]
        win = jnp.concatenate([_pair_heads_cols(w[:, 0:512]), w[:, 512:1408], kr, kr,
                               jnp.zeros((D_MODEL, 64), w.dtype)], axis=1).astype(BF16)
        uq = p['b_w_uq'][i].reshape(B_Q_LORA, B_HEADS, B_NOPE + B_ROPE)
        blocks = []
        for m in range(4):
            blocks += [uq[:, 2 * m, :B_NOPE], uq[:, 2 * m + 1, :B_NOPE],
                       uq[:, 2 * m, B_NOPE:], uq[:, 2 * m + 1, B_NOPE:],
                       jnp.zeros((B_Q_LORA, 64), uq.dtype)]
        wuq = jnp.concatenate(blocks, axis=1).astype(BF16)
        ukv = p['b_w_ukv'][i].reshape(B_KV_LORA, B_HEADS, B_NOPE + B_V)
        wukv = jnp.concatenate([ukv[:, :, :B_NOPE].reshape(B_KV_LORA, -1),
                                ukv[:, :, B_NOPE:].reshape(B_KV_LORA, -1)], axis=1).astype(BF16)
        wo = p['w_out_even'][i]
        wout = jnp.concatenate([_pair_heads_rows(wo[0:512]), wo[512:1024]], axis=0).astype(BF16)
        sink = jnp.broadcast_to(p['a_sink'][i].astype(F32)[:, None], (A_HEADS, LANES))
        even.append(dict(win=win, wuq=wuq, wukv=wukv, wout=wout, sink=sink,
                         gq=row(p['b_q_norm'][i]), gkv=row(p['b_kv_norm'][i])))
    for i in range(DEPTH // 2):
        w = p['w_in_odd'][i]
        win = jnp.concatenate([_pair_heads_cols(w[:, 0:512]), w[:, 512:]], axis=1).astype(BF16)
        gqk = jnp.concatenate([jnp.tile(p['c_q_norm'][i], C_HEADS),
                               jnp.tile(p['c_k_norm'][i], C_KV_HEADS)]).reshape(1, -1).astype(F32)
        wo = p['w_out_odd'][i]
        wout = jnp.concatenate([_pair_heads_rows(wo[0:512]), wo[512:1024]], axis=0).astype(BF16)
        odd.append(dict(win=win, gqk=gqk, wout=wout, lam=p['d_lambda'][i].astype(F32),
                        gsub=row(p['d_subln'][i])))
    L['even'], L['odd'] = even, odd
    seg = jnp.arange(640) // HEAD_DIM
    L['ones'] = (seg[:, None] == seg[None, :]).astype(BF16)
    L['x_wq'] = [p['x_wq'][l].astype(BF16) for l in range(DEPTH)]
    L['x_wkv'] = [p['x_wkv'][l].astype(BF16) for l in range(DEPTH)]
    L['x_wo'] = [p['x_wo'][l].astype(BF16) for l in range(DEPTH)]
    wr3 = []
    for l in range(DEPTH):
        wt = p['router'][l].T.astype(F32)
        w1 = wt.astype(BF16)
        r1 = wt - w1.astype(F32)
        w2 = r1.astype(BF16)
        w3 = (r1 - w2.astype(F32)).astype(BF16)
        wr3.append(jnp.stack([w1, w2, w3]))
    L['router'] = wr3
    L['norm_final'] = row(p['norm_final'])
    return L


def _tile(n, pref):
    t = min(n, pref)
    assert n % t == 0, (n, t)
    return t


def _trunk(x, mem, p, L):
    B, S, D = x.shape
    n_mem = mem.shape[1]
    T = B * S
    tm_pre = _tile(S, 512)
    tm_post = _tile(S, 256)
    tq = _tile(S, 256)
    cap = EC_CAPACITY * T // N_EXPERTS
    tc = _tile(cap, 1024)
    tf = 512
    tabs = _tables(S, tq, p['rel_bias'])
    xf = x.reshape(T, D).astype(F32)
    memf = mem.reshape(B * n_mem, D).astype(F32)
    tm_mem = _tile(B * n_mem, 256)
    mla_a = ((0, 64), (128, 160))
    mla_b = ((64, 128), (160, 192))
    for layer in range(DEPTH):
        i = layer // 2
        if layer % 2 == 0:
            w = L['even'][i]
            qa, ka, va, qcat, kcat, vb = _pre_even(xf, S, L['norm_mix'][layer], w['win'], w['gq'], w['gkv'],
                                                   w['wuq'], w['wukv'], tabs['tq'], tabs['tk'], tm_pre)
            o1 = _window_attn(qa, ka, va, tabs['a_bias'], w['sink'], B, S)
            o2 = _pair_attn(qcat, kcat, vb, B, S, tq, 256, False, mla_a, mla_b, "latent_attn")
        else:
            w = L['odd'][i]
            qc, kc, vc, qd, kd, vd = _pre_odd(xf, S, L['norm_mix'][layer], w['win'], w['gqk'], L['ones'],
                                              tabs['tax'], tm_pre)
            o1 = _pair_attn(qc, kc, vc, B, S, tq, 128, True, ((0, 64),), ((64, 128),), "axial_attn")
            lambda_init = 0.8 - 0.6 * math.exp(-0.3 * layer)
            o2 = _diff_attn(qd, kd, vd, w['lam'], w['gsub'], tabs['d_bias'], B, S, tq, lambda_init)
        kv = _norm_matmul(memf, L['norm_mem'][layer], L['x_wkv'][layer], tm_mem, BF16)
        x2, hn, aff = _post(xf, o1, o2, w['wout'], L['norm_cross'][layer], L['x_wq'][layer], kv,
                            L['x_wo'][layer], L['norm_ffn'][layer], L['router'][layer], S, n_mem, tm_post)
        xf = _moe(x2, hn, aff, p['e_w_gate'], p['e_w_up'], p['e_w_down'], layer, cap, tc, tf)
    out = _final_norm(xf, L['norm_final'], tm_pre)
    return out.reshape(B, S, D)


def kernel(x_prompt, x_sample, mem_prompt, mem_sample, rel_bias, norm_mix, norm_cross, norm_mem, norm_ffn,
           norm_final, w_in_even, a_sink, b_q_norm, b_kv_norm, b_w_uq, b_w_ukv, w_out_even, w_in_odd,
           c_q_norm, c_k_norm, d_lambda, d_subln, w_out_odd, x_wq, x_wkv, x_wo, router, e_w_gate,
           e_w_up, e_w_down):
    p = dict(rel_bias=rel_bias, norm_mix=norm_mix, norm_cross=norm_cross, norm_mem=norm_mem,
             norm_ffn=norm_ffn, norm_final=norm_final, w_in_even=w_in_even, a_sink=a_sink,
             b_q_norm=b_q_norm, b_kv_norm=b_kv_norm, b_w_uq=b_w_uq, b_w_ukv=b_w_ukv,
             w_out_even=w_out_even, w_in_odd=w_in_odd, c_q_norm=c_q_norm, c_k_norm=c_k_norm,
             d_lambda=d_lambda, d_subln=d_subln, w_out_odd=w_out_odd, x_wq=x_wq, x_wkv=x_wkv,
             x_wo=x_wo, router=router, e_w_gate=e_w_gate, e_w_up=e_w_up, e_w_down=e_w_down)
    L = _prep_weights(p)
    y_prompt = _trunk(x_prompt, mem_prompt, p, L)
    y_sample = _trunk(x_sample, mem_sample, p, L)
    return (y_prompt, y_sample)
```

```python
import functools
import math

import jax
import jax.numpy as jnp
from jax import lax
from jax.experimental import pallas as pl
from jax.experimental.pallas import tpu as pltpu

F32 = jnp.float32
BF16 = jnp.bfloat16

D_MODEL = 1024
DEPTH = 4
HEAD_DIM = 64
QB = 128
WINDOW = 128
A_HEADS = 8
A_KV_HEADS = 2
B_HEADS = 8
B_Q_LORA = 384
B_KV_LORA = 256
B_NOPE = 64
B_ROPE = 32
B_V = 64
C_HEADS = 8
C_KV_HEADS = 2
D_HEADS = 4
X_HEADS = 4
X_HEAD_DIM = D_MODEL // X_HEADS
N_BUCKETS = 32
MAX_DISTANCE = 128
N_EXPERTS = 16
EXPERT_FF = 2 * D_MODEL
EC_CAPACITY = 2
GRID_W = 64
ROPE_BASE = 10000.0
NEG_INF = -1e30
EPS = 1e-6
LOG2E = math.log2(math.e)

LANES = 128
VMEM_LIMIT = 48 * 1024 * 1024

_NT = (((1,), (1,)), ((), ()))


def _cparams():
    return pltpu.CompilerParams(vmem_limit_bytes=VMEM_LIMIT)


def _rms(x, g):
    ms = jnp.mean(x * x, axis=-1, keepdims=True)
    return x * lax.rsqrt(ms + EPS) * g


def _rope(x, tab_ref):
    w = x.shape[1]
    return (x * tab_ref[0]
            + pltpu.roll(x, w - 16, 1) * tab_ref[1]
            + pltpu.roll(x, 16, 1) * tab_ref[2])


def _softmax_rows(s):
    m = jnp.max(s, axis=-1, keepdims=True)
    e = jnp.exp(s - m)
    return e, jnp.sum(e, axis=-1, keepdims=True)


def _pre_even_kernel(x_ref, g_ref, win_ref, gq_ref, gkv_ref, wuq_ref, wukv_ref, tq_ref, tk_ref,
                     qa_ref, ka_ref, va_ref, qcat_ref, kcat_ref, vb_ref):
    hn = _rms(x_ref[...], g_ref[...]).astype(BF16)
    z = jnp.dot(hn, win_ref[...], preferred_element_type=F32)
    qa_ref[...] = (z[:, 0:512] * (HEAD_DIM ** -0.5)).astype(BF16)
    ka_ref[...] = z[:, 512:640].astype(BF16)
    va_ref[...] = z[:, 640:768].astype(BF16)
    cq = _rms(z[:, 768:1152], gq_ref[...]).astype(BF16)
    q = jnp.dot(cq, wuq_ref[...], preferred_element_type=F32)
    for m in range(4):
        qcat_ref[:, 256 * m:256 * (m + 1)] = _rope(q[:, 256 * m:256 * (m + 1)], tq_ref).astype(BF16)
    ckv = _rms(z[:, 1152:1408], gkv_ref[...]).astype(BF16)
    kv = jnp.dot(ckv, wukv_ref[...], preferred_element_type=F32)
    kr = _rope(z[:, 1408:1536], tk_ref).astype(BF16)
    for m in range(4):
        kcat_ref[:, 256 * m:256 * m + 128] = kv[:, 128 * m:128 * (m + 1)].astype(BF16)
        kcat_ref[:, 256 * m + 128:256 * (m + 1)] = kr
    vb_ref[...] = kv[:, 512:1024].astype(BF16)


def _pre_odd_kernel(x_ref, g_ref, win_ref, gqk_ref, ones_ref, tab_ref,
                    qc_ref, kc_ref, vc_ref, qd_ref, kd_ref, vd_ref):
    hn = _rms(x_ref[...], g_ref[...]).astype(BF16)
    z = jnp.dot(hn, win_ref[...], preferred_element_type=F32)
    qk = z[:, 0:640]
    ss = qk * qk
    hi = ss.astype(BF16)
    lo = (ss - hi.astype(F32)).astype(BF16)
    ms = (jnp.dot(hi, ones_ref[...], preferred_element_type=F32)
          + jnp.dot(lo, ones_ref[...], preferred_element_type=F32)) * (1.0 / HEAD_DIM)
    qkn = qk * lax.rsqrt(ms + EPS) * gqk_ref[...]
    for m in range(4):
        blk = _rope(qkn[:, 128 * m:128 * (m + 1)], tab_ref)
        qc_ref[:, 128 * m:128 * (m + 1)] = (blk * (LOG2E * HEAD_DIM ** -0.5)).astype(BF16)
    kc_ref[...] = _rope(qkn[:, 512:640], tab_ref).astype(BF16)
    vc_ref[...] = z[:, 640:768].astype(BF16)
    qd_ref[...] = (z[:, 768:1280] * (LOG2E * HEAD_DIM ** -0.5)).astype(BF16)
    kd_ref[...] = z[:, 1280:1792].astype(BF16)
    vd_ref[...] = z[:, 1792:2304].astype(BF16)


def _const_spec(shape):
    nd = len(shape)
    return pl.BlockSpec(shape, lambda *_: (0,) * nd)


def _pre_even(x, S, g, win, gq, gkv, wuq, wukv, tq, tk, tm):
    T = x.shape[0]
    ns = S // tm
    row = lambda w: pl.BlockSpec((tm, w), lambda i: (i, 0))
    tab = lambda w: pl.BlockSpec((3, tm, w), lambda i: (0, i % ns, 0))
    out_w = (512, 128, 128, 1024, 1024, 512)
    return pl.pallas_call(
        _pre_even_kernel,
        grid=(T // tm,),
        in_specs=[row(D_MODEL), _const_spec(g.shape), _const_spec(win.shape), _const_spec(gq.shape),
                  _const_spec(gkv.shape), _const_spec(wuq.shape), _const_spec(wukv.shape),
                  tab(256), tab(128)],
        out_specs=[row(w) for w in out_w],
        out_shape=[jax.ShapeDtypeStruct((T, w), BF16) for w in out_w],
        compiler_params=_cparams(),
        name="pre_even",
    )(x, g, win, gq, gkv, wuq, wukv, tq, tk)


def _pre_odd(x, S, g, win, gqk, ones, tab3, tm):
    T = x.shape[0]
    ns = S // tm
    row = lambda w: pl.BlockSpec((tm, w), lambda i: (i, 0))
    out_w = (512, 128, 128, 512, 512, 512)
    return pl.pallas_call(
        _pre_odd_kernel,
        grid=(T // tm,),
        in_specs=[row(D_MODEL), _const_spec(g.shape), _const_spec(win.shape), _const_spec(gqk.shape),
                  _const_spec(ones.shape), pl.BlockSpec((3, tm, 128), lambda i: (0, i % ns, 0))],
        out_specs=[row(w) for w in out_w],
        out_shape=[jax.ShapeDtypeStruct((T, w), BF16) for w in out_w],
        compiler_params=_cparams(),
        name="pre_odd",
    )(x, g, win, gqk, ones, tab3)


def _window_kernel(q_ref, kp_ref, kc_ref, kn_ref, vp_ref, vc_ref, vn_ref, bias_ref, sink_ref, o_ref):
    i = pl.program_id(1)
    nb = pl.num_programs(1)
    q = q_ref[...]
    kw = jnp.concatenate([kp_ref[...], kc_ref[...], kn_ref[...]], axis=0)
    vw = jnp.concatenate([vp_ref[...], vc_ref[...], vn_ref[...]], axis=0)
    col = lax.broadcasted_iota(jnp.int32, (1, 3 * QB), 1)
    outside = ((col < QB) & (i == 0)) | ((col >= 2 * QB) & (i == nb - 1))
    lane = lax.broadcasted_iota(jnp.int32, (1, LANES), 1)
    for m in range(4):
        qblk = q[:, 128 * m:128 * (m + 1)]
        outs = []
        for half in range(2):
            h = m + 4 * half
            keep = (lane < HEAD_DIM) if half == 0 else (lane >= HEAD_DIM)
            qm = jnp.where(keep, qblk, jnp.zeros_like(qblk))
            s = lax.dot_general(qm, kw, _NT, preferred_element_type=F32) + bias_ref[h]
            s = jnp.where(outside, NEG_INF, s)
            sk = sink_ref[h:h + 1, 0:1]
            mx = jnp.maximum(jnp.max(s, axis=-1, keepdims=True), sk)
            e = jnp.exp(s - mx)
            den = jnp.sum(e, axis=-1, keepdims=True) + jnp.exp(sk - mx)
            outs.append(jnp.dot(e.astype(BF16), vw, preferred_element_type=F32) / den)
        o_ref[:, 128 * m:128 * (m + 1)] = jnp.where(lane < HEAD_DIM, outs[0], outs[1]).astype(BF16)


def _window_attn(qa, ka, va, bias, sink, B, S):
    T = qa.shape[0]
    nb = S // QB
    kv_spec = lambda d: pl.BlockSpec(
        (QB, 128), lambda b, i: (b * nb + jnp.clip(i + d, 0, nb - 1), 0))
    return pl.pallas_call(
        _window_kernel,
        grid=(B, nb),
        in_specs=[pl.BlockSpec((QB, 512), lambda b, i: (b * nb + i, 0)),
                  kv_spec(-1), kv_spec(0), kv_spec(1), kv_spec(-1), kv_spec(0), kv_spec(1),
                  _const_spec(bias.shape), _const_spec(sink.shape)],
        out_specs=pl.BlockSpec((QB, 512), lambda b, i: (b * nb + i, 0)),
        out_shape=jax.ShapeDtypeStruct((T, 512), BF16),
        compiler_params=_cparams(),
        name="window_attn",
    )(qa, ka, ka, ka, va, va, va, bias, sink)


def _pair_kernel(q_ref, k_ref, v_ref, o_ref, vx_ref, *, lanes_a, lanes_b, kc):
    @pl.when(pl.program_id(2) == 0)
    def _():
        vx_ref[:, 0:LANES] = v_ref[...]
        vx_ref[:, LANES:2 * LANES] = jnp.ones((vx_ref.shape[0], LANES), BF16)

    q = q_ref[...]
    tq, wq = q.shape
    S = k_ref.shape[0]
    lane = lax.broadcasted_iota(jnp.int32, (1, wq), 1)
    qms = []
    for ranges in (lanes_a, lanes_b):
        keep = functools.reduce(jnp.logical_or, [(lane >= lo) & (lane < hi) for lo, hi in ranges])
        qms.append(jnp.where(keep, q, jnp.zeros_like(q)))
    m = [jnp.full((tq, 1), NEG_INF, F32) for _ in range(2)]
    acc = [jnp.zeros((tq, 2 * LANES), F32) for _ in range(2)]
    for c in range(S // kc):
        kchunk = k_ref[kc * c:kc * (c + 1), :]
        vchunk = vx_ref[kc * c:kc * (c + 1), :]
        for h in range(2):
            s = lax.dot_general(qms[h], kchunk, _NT, preferred_element_type=F32)
            m_new = jnp.maximum(m[h], jnp.max(s, axis=-1, keepdims=True))
            p = jnp.exp2(s - m_new).astype(BF16)
            acc[h] = jnp.exp2(m[h] - m_new) * acc[h] + jnp.dot(p, vchunk, preferred_element_type=F32)
            m[h] = m_new
    outs = [a[:, 0:LANES] / a[:, LANES:LANES + 1] for a in acc]
    lane_o = lax.broadcasted_iota(jnp.int32, (1, LANES), 1)
    o_ref[...] = jnp.where(lane_o < 64, outs[0], outs[1]).astype(BF16)


def _pair_attn(q, k, v, B, S, tq, wq, shared_kv, lanes_a, lanes_b, name):
    T = q.shape[0]
    nq = S // tq
    kc = _tile(S, 512)
    kv_map = (lambda b, m, i: (b, 0)) if shared_kv else (lambda b, m, i: (b, m))
    return pl.pallas_call(
        functools.partial(_pair_kernel, lanes_a=lanes_a, lanes_b=lanes_b, kc=kc),
        scratch_shapes=[pltpu.VMEM((S, 2 * LANES), BF16)],
        grid=(B, 4, nq),
        in_specs=[pl.BlockSpec((tq, wq), lambda b, m, i: (b * nq + i, m)),
                  pl.BlockSpec((S, wq), kv_map),
                  pl.BlockSpec((S, 128), kv_map)],
        out_specs=pl.BlockSpec((tq, 128), lambda b, m, i: (b * nq + i, m)),
        out_shape=jax.ShapeDtypeStruct((T, 512), BF16),
        compiler_params=_cparams(),
        name=name,
    )(q, k, v)


def _diff_kernel(lam_ref, gsub_ref, q_ref, k_ref, v_ref, bias_ref, o_ref, s_scr, *, S, tq, lambda_init):
    i = pl.program_id(2)
    r = tq // QB
    q = q_ref[...]
    k = k_ref[...]
    lp = lam_ref[...]
    lam = (jnp.exp(jnp.sum(lp[0:1] * lp[1:2], axis=-1, keepdims=True))
           - jnp.exp(jnp.sum(lp[2:3] * lp[3:4], axis=-1, keepdims=True)) + lambda_init)
    lane = lax.broadcasted_iota(jnp.int32, (1, LANES), 1)
    probs = []
    for c in range(2):
        keep = (lane < HEAD_DIM) if c == 0 else (lane >= HEAD_DIM)
        qm = jnp.where(keep, q, jnp.zeros_like(q))
        s_scr[...] = lax.dot_general(qm, k, _NT, preferred_element_type=F32)
        for j in range(S // QB):
            u = jnp.clip(j - r * i + 2, 0, r + 3)
            s_scr[:, QB * j:QB * (j + 1)] = s_scr[:, QB * j:QB * (j + 1)] + bias_ref[0, u]
        s = s_scr[...]
        e = jnp.exp2(s - jnp.max(s, axis=-1, keepdims=True))
        probs.append(e * (1.0 / jnp.sum(e, axis=-1, keepdims=True)))
    a = (probs[0] - lam * probs[1]).astype(BF16)
    o = jnp.dot(a, v_ref[...], preferred_element_type=F32)
    o_ref[...] = (_rms(o, gsub_ref[...]) * (1.0 - lambda_init)).astype(BF16)


def _diff_attn(qd, kd, vd, lam_p, gsub, bias, B, S, tq, lambda_init):
    T = qd.shape[0]
    nq = S // tq
    nt = bias.shape[1]
    return pl.pallas_call(
        functools.partial(_diff_kernel, S=S, tq=tq, lambda_init=lambda_init),
        grid=(B, D_HEADS, nq),
        in_specs=[_const_spec(lam_p.shape), _const_spec(gsub.shape),
                  pl.BlockSpec((tq, 128), lambda b, h, i: (b * nq + i, h)),
                  pl.BlockSpec((S, 128), lambda b, h, i: (b, h)),
                  pl.BlockSpec((S, 128), lambda b, h, i: (b, h)),
                  pl.BlockSpec((1, nt, tq, 128), lambda b, h, i: (h, 0, 0, 0))],
        out_specs=pl.BlockSpec((tq, 128), lambda b, h, i: (b * nq + i, h)),
        out_shape=jax.ShapeDtypeStruct((T, 512), BF16),
        scratch_shapes=[pltpu.VMEM((tq, S), F32)],
        compiler_params=_cparams(),
        name="diff_attn",
    )(lam_p, gsub, qd, kd, vd, bias)


def _norm_matmul_kernel(x_ref, g_ref, w_ref, o_ref):
    hn = _rms(x_ref[...], g_ref[...]).astype(BF16)
    o_ref[...] = jnp.dot(hn, w_ref[...], preferred_element_type=F32).astype(o_ref.dtype)


def _norm_matmul(x, g, w, tm, out_dtype):
    T, K = x.shape
    N = w.shape[1]
    return pl.pallas_call(
        _norm_matmul_kernel,
        grid=(T // tm,),
        in_specs=[pl.BlockSpec((tm, K), lambda i: (i, 0)), _const_spec(g.shape), _const_spec(w.shape)],
        out_specs=pl.BlockSpec((tm, N), lambda i: (i, 0)),
        out_shape=jax.ShapeDtypeStruct((T, N), out_dtype),
        compiler_params=_cparams(),
        name="norm_matmul",
    )(x, g, w)


def _post_kernel(x_ref, oa_ref, ob_ref, wout_ref, gx_ref, wq_ref, kv_ref, wo_ref, gf_ref, wr_ref,
                 x2_ref, hn_ref, aff_ref, afft_ref):
    x1 = (x_ref[...]
          + jnp.dot(oa_ref[...], wout_ref[0:512, :], preferred_element_type=F32)
          + jnp.dot(ob_ref[...], wout_ref[512:1024, :], preferred_element_type=F32))
    hc = _rms(x1, gx_ref[...]).astype(BF16)
    q = (jnp.dot(hc, wq_ref[...], preferred_element_type=F32) * (X_HEAD_DIM ** -0.5)).astype(BF16)
    heads = []
    for h in range(X_HEADS):
        lo = X_HEAD_DIM * h
        kh = kv_ref[:, lo:lo + X_HEAD_DIM]
        vh = kv_ref[:, D_MODEL + lo:D_MODEL + lo + X_HEAD_DIM]
        s = lax.dot_general(q[:, lo:lo + X_HEAD_DIM], kh, _NT, preferred_element_type=F32)
        e, l = _softmax_rows(s)
        heads.append((jnp.dot(e.astype(BF16), vh, preferred_element_type=F32) / l).astype(BF16))
    o = jnp.concatenate(heads, axis=1)
    x2 = x1 + jnp.dot(o, wo_ref[...], preferred_element_type=F32)
    x2_ref[...] = x2
    hf = _rms(x2, gf_ref[...])
    hn_ref[...] = hf.astype(BF16)
    tm = hf.shape[0]
    h_hi = hf.astype(BF16)
    h_lo = (hf - h_hi.astype(F32)).astype(BF16)
    r = jnp.dot(jnp.concatenate([h_hi, h_lo], axis=0), wr_ref[...], preferred_element_type=F32)
    top = r[0:tm]
    logits = top + pltpu.roll(top, LANES - N_EXPERTS, 1) + r[tm:2 * tm]
    valid = lax.broadcasted_iota(jnp.int32, (1, LANES), 1) < N_EXPERTS
    mx = jnp.max(jnp.where(valid, logits, NEG_INF), axis=-1, keepdims=True)
    e = jnp.where(valid, jnp.exp(logits - mx), 0.0)
    aff = e / jnp.sum(e, axis=-1, keepdims=True)
    afft_ref[...] = aff[:, 0:N_EXPERTS]
    aff_t = aff.T
    for c in range(tm // LANES):
        aff_ref[c] = aff_t[0:N_EXPERTS, LANES * c:LANES * (c + 1)]


def _post(x, oa, ob, wout, gx, wq, kv, wo, gf, wr3, S, n_mem, tm):
    T = x.shape[0]
    per_b = S // tm
    row = lambda w: pl.BlockSpec((tm, w), lambda i: (i, 0))
    return pl.pallas_call(
        _post_kernel,
        grid=(T // tm,),
        in_specs=[row(D_MODEL), row(512), row(512), _const_spec(wout.shape), _const_spec(gx.shape),
                  _const_spec(wq.shape), pl.BlockSpec((n_mem, 2 * D_MODEL), lambda i: (i // per_b, 0)),
                  _const_spec(wo.shape), _const_spec(gf.shape), _const_spec(wr3.shape)],
        out_specs=[row(D_MODEL), row(D_MODEL),
                   pl.BlockSpec((tm // LANES, N_EXPERTS, LANES), lambda i: (i, 0, 0)), row(N_EXPERTS)],
        out_shape=[jax.ShapeDtypeStruct((T, D_MODEL), F32), jax.ShapeDtypeStruct((T, D_MODEL), BF16),
                   jax.ShapeDtypeStruct((T // LANES, N_EXPERTS, LANES), F32),
                   jax.ShapeDtypeStruct((T, N_EXPERTS), F32)],
        compiler_params=_cparams(),
        name="post",
    )(x, oa, ob, wout, gx, wq, kv, wo, gf, wr3)


def _route_kernel(aff_ref, tri_ref, ones_ref, carrymat_ref, key_ref, carry_ref, *, cap):
    x = aff_ref[...]
    nj = x.shape[0]

    def count(mask):
        c = jnp.sum(mask.astype(F32), axis=0, keepdims=True)
        return jnp.sum(c, axis=2, keepdims=True)

    def bisect(b, thr):
        cand = thr | lax.shift_left(jnp.int32(1), 30 - b)
        return jnp.where(count(x >= pltpu.bitcast(cand, F32)) >= cap, cand, thr)

    thr = pltpu.bitcast(lax.fori_loop(0, 31, bisect, jnp.zeros((1, N_EXPERTS, 1), jnp.int32)), F32)
    gt = x > thr
    eq = x == thr
    need = cap - count(gt)

    def prefix(mask):
        v = mask.astype(BF16).reshape(nj * N_EXPERTS, LANES)
        within = jnp.dot(v, tri_ref[...], preferred_element_type=F32)
        tot = jnp.dot(v, ones_ref[...], preferred_element_type=F32)
        carry = jnp.dot(carrymat_ref[...], tot.astype(BF16), preferred_element_type=F32)
        return within + carry, carry

    eq_rank, _ = prefix(eq)
    sel = gt | (eq & (eq_rank.reshape(nj, N_EXPERTS, LANES) < need))
    rank, carry = prefix(sel)
    key_ref[...] = jnp.where(sel, rank.reshape(nj, N_EXPERTS, LANES), -1.0)
    carry_ref[...] = carry


def _route(aff3, tri, ones, carrymat, cap):
    nj = aff3.shape[0]
    rows = nj * N_EXPERTS
    return pl.pallas_call(
        functools.partial(_route_kernel, cap=cap),
        out_shape=[jax.ShapeDtypeStruct((nj, N_EXPERTS, LANES), F32),
                   jax.ShapeDtypeStruct((rows, LANES), F32)],
        compiler_params=_cparams(),
        name="route",
    )(aff3, tri, ones, carrymat)


def _window_start(c0, cap, win):
    a = jnp.minimum((c0 // 16) * 16, cap - win)
    return pl.multiple_of(a, 16)


GATHER_EXPERTS = 2


def _gather_kernel(c0_ref, key_ref, hn_ref, xe_ref, *, cap, win, nq):
    sb = pl.program_id(1)

    @pl.when(sb == 0)
    def _():
        xe_ref[...] = jnp.zeros_like(xe_ref)

    slot = lax.broadcasted_iota(jnp.int32, (win, 1), 0).astype(F32)

    def add_window(j, q, a, lo):
        keyrow = jnp.concatenate([key_ref[j, 2 * q:2 * q + 1, :], key_ref[j, 2 * q + 1:2 * q + 2, :]], axis=1)
        hit = (keyrow - a.astype(F32) == slot) & (keyrow >= lo.astype(F32))
        rows = jnp.dot(hit.astype(BF16), hn_ref[256 * q:256 * (q + 1), :],
                       preferred_element_type=F32)
        xe_ref[j, pl.ds(a, win), :] = xe_ref[j, pl.ds(a, win), :] + rows.astype(BF16)

    experts = [pl.program_id(0) * GATHER_EXPERTS + j for j in range(GATHER_EXPERTS)]
    starts = [[_window_start(c0_ref[e, sb * nq + q], cap, win) for q in range(nq)] for e in experts]
    for q in range(nq):
        for j, e in enumerate(experts):
            add_window(j, q, starts[j][q], c0_ref[e, sb * nq + q])
    for q in range(nq):
        for j, e in enumerate(experts):
            nxt = starts[j][q] + win
            pl.when(c0_ref[e, sb * nq + q + 1] > nxt)(
                functools.partial(add_window, j, q, _window_start(nxt, cap, win), nxt))


def _gather(c0, key_em, hn, cap, win, tsb):
    T, D = hn.shape
    nq = tsb // 256
    ge = GATHER_EXPERTS
    grid_spec = pltpu.PrefetchScalarGridSpec(
        num_scalar_prefetch=1,
        grid=(N_EXPERTS // ge, T // tsb),
        in_specs=[pl.BlockSpec((ge, tsb // LANES, LANES), lambda e, sb, c0: (e, sb, 0)),
                  pl.BlockSpec((tsb, D), lambda e, sb, c0: (sb, 0))],
        out_specs=pl.BlockSpec((ge, cap, D), lambda e, sb, c0: (e, 0, 0)),
    )
    return pl.pallas_call(
        functools.partial(_gather_kernel, cap=cap, win=win, nq=nq),
        grid_spec=grid_spec,
        out_shape=jax.ShapeDtypeStruct((N_EXPERTS, cap, D), BF16),
        compiler_params=_cparams(),
        name="moe_gather",
    )(c0, key_em, hn)


def _ffn_kernel(xe_ref, wg_ref, wu_ref, wd_ref, y_ref, acc_ref):
    f = pl.program_id(2)

    @pl.when(f == 0)
    def _():
        acc_ref[...] = jnp.zeros_like(acc_ref)

    xe = xe_ref[0]
    a = jnp.dot(xe, wg_ref[0, 0].astype(BF16), preferred_element_type=F32)
    u = jnp.dot(xe, wu_ref[0, 0].astype(BF16), preferred_element_type=F32)
    h = (a * (1.0 / (1.0 + jnp.exp(-a))) * u).astype(BF16)
    acc_ref[...] += jnp.dot(h, wd_ref[0, 0].astype(BF16), preferred_element_type=F32)

    @pl.when(f == pl.num_programs(2) - 1)
    def _():
        y_ref[0] = acc_ref[...].astype(BF16)


def _expert_ffn(xe, wg, wu, wd, layer, tc, tf):
    E, cap, D = xe.shape
    F = wg.shape[3]
    return pl.pallas_call(
        _ffn_kernel,
        grid=(E, cap // tc, F // tf),
        in_specs=[pl.BlockSpec((1, tc, D), lambda e, c, f: (e, c, 0)),
                  pl.BlockSpec((1, 1, D, tf), lambda e, c, f: (layer, e, 0, f)),
                  pl.BlockSpec((1, 1, D, tf), lambda e, c, f: (layer, e, 0, f)),
                  pl.BlockSpec((1, 1, tf, D), lambda e, c, f: (layer, e, f, 0))],
        out_specs=pl.BlockSpec((1, tc, D), lambda e, c, f: (e, c, 0)),
        out_shape=jax.ShapeDtypeStruct((E, cap, D), BF16),
        scratch_shapes=[pltpu.VMEM((tc, D), F32)],
        compiler_params=_cparams(),
        name="expert_ffn",
    )(xe, wg, wu, wd)


def _combine_kernel(c0_ref, x_ref, key_ref, aff_ref, y_ref, o_ref, *, cap, win, nq):
    sb = pl.program_id(0)
    e = pl.program_id(1)

    @pl.when(e == 0)
    def _():
        o_ref[...] = x_ref[...]

    mine = lax.broadcasted_iota(jnp.int32, (1, N_EXPERTS), 1) == e
    keycol = jnp.sum(jnp.where(mine, key_ref[...], 0.0), axis=-1, keepdims=True)
    gate = jnp.sum(jnp.where(mine, aff_ref[...], 0.0), axis=-1, keepdims=True)
    slot = lax.broadcasted_iota(jnp.int32, (1, win), 1).astype(F32)

    def add_window(q, a, lo):
        kq = keycol[256 * q:256 * (q + 1)]
        hit = (kq - a.astype(F32) == slot) & (kq >= lo.astype(F32))
        rows = jnp.dot(hit.astype(BF16), y_ref[0, pl.ds(a, win), :], preferred_element_type=F32)
        o_ref[256 * q:256 * (q + 1), :] = (o_ref[256 * q:256 * (q + 1), :]
                                           + rows * gate[256 * q:256 * (q + 1)])

    starts = [_window_start(c0_ref[e, sb * nq + q], cap, win) for q in range(nq)]
    for q in range(nq):
        add_window(q, starts[q], c0_ref[e, sb * nq + q])
    for q in range(nq):
        nxt = starts[q] + win
        pl.when(c0_ref[e, sb * nq + q + 1] > nxt)(
            functools.partial(add_window, q, _window_start(nxt, cap, win), nxt))


def _combine(c0, x, key_tm, aff_tm, y, cap, win, tsb):
    T, D = x.shape
    nq = tsb // 256
    grid_spec = pltpu.PrefetchScalarGridSpec(
        num_scalar_prefetch=1,
        grid=(T // tsb, N_EXPERTS),
        in_specs=[pl.BlockSpec((tsb, D), lambda sb, e, c0: (sb, 0)),
                  pl.BlockSpec((tsb, N_EXPERTS), lambda sb, e, c0: (sb, 0)),
                  pl.BlockSpec((tsb, N_EXPERTS), lambda sb, e, c0: (sb, 0)),
                  pl.BlockSpec((1, cap, D), lambda sb, e, c0: (e, 0, 0))],
        out_specs=pl.BlockSpec((tsb, D), lambda sb, e, c0: (sb, 0)),
    )
    return pl.pallas_call(
        functools.partial(_combine_kernel, cap=cap, win=win, nq=nq),
        grid_spec=grid_spec,
        out_shape=jax.ShapeDtypeStruct((T, D), F32),
        compiler_params=_cparams(),
        name="moe_combine",
    )(c0, x, key_tm, aff_tm, y)


def _route_constants(nj):
    lane = jnp.arange(LANES)
    tri = (lane[:, None] < lane[None, :]).astype(BF16)
    ones = jnp.ones((LANES, LANES), BF16)
    r = jnp.arange(nj * N_EXPERTS)
    carrymat = ((r[None, :] % N_EXPERTS == r[:, None] % N_EXPERTS) & (r[None, :] < r[:, None])).astype(BF16)
    return tri, ones, carrymat


def _moe(x2, hn, aff3, aff_tm, wg, wu, wd, layer, cap, tc, tf):
    T, D = x2.shape
    nj = T // LANES
    win = min(256, cap)
    tsb = min(2048, T)
    key3, carry = _route(aff3, *_route_constants(nj), cap)
    key_em = key3.transpose(1, 0, 2)
    key_tm = key3.transpose(0, 2, 1).reshape(T, N_EXPERTS)
    starts = carry[:, 0].reshape(nj, N_EXPERTS)[::2].T.astype(jnp.int32)
    c0 = jnp.concatenate([starts, jnp.full((N_EXPERTS, 1), cap, jnp.int32)], axis=1)
    xe = _gather(c0, key_em, hn, cap, win, tsb)
    y = _expert_ffn(xe, wg, wu, wd, layer, tc, tf)
    return _combine(c0, x2, key_tm, aff_tm, y, cap, win, tsb)


def _final_norm_kernel(x_ref, g_ref, o_ref):
    o_ref[...] = _rms(x_ref[...], g_ref[...])


def _final_norm(x, g, tm):
    T, D = x.shape
    return pl.pallas_call(
        _final_norm_kernel,
        grid=(T // tm,),
        in_specs=[pl.BlockSpec((tm, D), lambda i: (i, 0)), _const_spec(g.shape)],
        out_specs=pl.BlockSpec((tm, D), lambda i: (i, 0)),
        out_shape=jax.ShapeDtypeStruct((T, D), F32),
        name="final_norm",
    )(x, g)


def _t5_bucket(rel):
    half = N_BUCKETS // 2
    exact = half // 2
    n = jnp.abs(rel)
    big = exact + (jnp.log(jnp.maximum(n, 1).astype(F32) / exact)
                   / math.log(MAX_DISTANCE / exact) * (half - exact)).astype(jnp.int32)
    big = jnp.minimum(big, half - 1)
    return jnp.where(rel > 0, half, 0) + jnp.where(n < exact, n, big)


def _rope_freqs(pos, dim):
    freqs = ROPE_BASE ** (-jnp.arange(0, dim, 2, dtype=F32) / dim)
    ang = pos.astype(F32)[:, None] * freqs[None, :]
    return jnp.cos(ang), jnp.sin(ang)


def _rope_lane_table(cos, sin, active):
    w = cos.shape[1]
    first = (jnp.arange(w) % 32) < 16
    c = jnp.where(active[None, :], cos, 1.0)
    s1 = jnp.where((active & first)[None, :], -sin, 0.0)
    s2 = jnp.where((active & ~first)[None, :], sin, 0.0)
    return jnp.stack([c, s1, s2]).astype(F32)


def _pair_heads_cols(w):
    k = w.shape[0]
    return w.reshape(k, 2, 4, HEAD_DIM).transpose(0, 2, 1, 3).reshape(k, 8 * HEAD_DIM)


def _pair_heads_rows(w):
    n = w.shape[1]
    return w.reshape(2, 4, HEAD_DIM, n).transpose(1, 0, 2, 3).reshape(8 * HEAD_DIM, n)


def _bias_lookup(table, bucket):
    shape = (table.shape[1],) + (1,) * bucket.ndim
    out = jnp.zeros((table.shape[1],) + bucket.shape, F32)
    for b in range(N_BUCKETS):
        out = jnp.where(bucket[None] == b, table[b].reshape(shape), out)
    return out


def _tables(S, tq_diff, rel_bias):
    t = jnp.arange(S)
    lane128 = jnp.arange(128)
    lane256 = jnp.arange(256)
    cos_t, sin_t = _rope_freqs(t, B_ROPE)
    scale_b = LOG2E * (B_NOPE + B_ROPE) ** -0.5
    tq = _rope_lane_table(jnp.tile(cos_t, (1, 16)), jnp.tile(sin_t, (1, 16)),
                          (lane256 >= 128) & (lane256 < 192)) * scale_b
    tk = _rope_lane_table(jnp.tile(cos_t, (1, 8)), jnp.tile(sin_t, (1, 8)), lane128 < 64)
    cos_r, sin_r = _rope_freqs(t // GRID_W, HEAD_DIM // 2)
    cos_c, sin_c = _rope_freqs(t % GRID_W, HEAD_DIM // 2)
    cos_ax = jnp.tile(jnp.concatenate([cos_r, cos_r, cos_c, cos_c], axis=1), (1, 2))
    sin_ax = jnp.tile(jnp.concatenate([sin_r, sin_r, sin_c, sin_c], axis=1), (1, 2))
    tax = _rope_lane_table(cos_ax, sin_ax, jnp.ones((128,), bool))
    table = rel_bias.astype(F32)
    rel_win = jnp.arange(3 * QB)[None, :] - QB - jnp.arange(QB)[:, None]
    a_bias = _bias_lookup(table[:, :A_HEADS], _t5_bucket(rel_win))
    a_bias = jnp.where((jnp.abs(rel_win) <= WINDOW)[None], a_bias, NEG_INF)
    r = tq_diff // QB
    rel_d = (QB * (jnp.arange(r + 4)[:, None, None] - 2)
             + jnp.arange(QB)[None, None, :] - jnp.arange(tq_diff)[None, :, None])
    d_bias = _bias_lookup(table[:, A_HEADS:], _t5_bucket(rel_d)) * LOG2E
    return dict(tq=tq, tk=tk, tax=tax, a_bias=a_bias, d_bias=d_bias)


def _prep_weights(p):
    row = lambda v: v.reshape(1, -1).astype(F32)
    L = {}
    L['norm_mix'] = [row(p['norm_mix'][l]) for l in range(DEPTH)]
    L['norm_cross'] = [row(p['norm_cross'][l]) for l in range(DEPTH)]
    L['norm_mem'] = [row(p['norm_mem'][l]) for l in range(DEPTH)]
    L['norm_ffn'] = [row(p['norm_ffn'][l]) for l in range(DEPTH)]
    even, odd = [], []
    for i in range((DEPTH + 1) // 2):
        w = p['w_in_even'][i]
        kr = w[:, 1408:1440]
        win = jnp.concatenate([_pair_heads_cols(w[:, 0:512]), w[:, 512:1408], kr, kr,
                               jnp.zeros((D_MODEL, 64), w.dtype)], axis=1).astype(BF16)
        uq = p['b_w_uq'][i].reshape(B_Q_LORA, B_HEADS, B_NOPE + B_ROPE)
        blocks = []
        for m in range(4):
            blocks += [uq[:, 2 * m, :B_NOPE], uq[:, 2 * m + 1, :B_NOPE],
                       uq[:, 2 * m, B_NOPE:], uq[:, 2 * m + 1, B_NOPE:],
                       jnp.zeros((B_Q_LORA, 64), uq.dtype)]
        wuq = jnp.concatenate(blocks, axis=1).astype(BF16)
        ukv = p['b_w_ukv'][i].reshape(B_KV_LORA, B_HEADS, B_NOPE + B_V)
        wukv = jnp.concatenate([ukv[:, :, :B_NOPE].reshape(B_KV_LORA, -1),
                                ukv[:, :, B_NOPE:].reshape(B_KV_LORA, -1)], axis=1).astype(BF16)
        wo = p['w_out_even'][i]
        wout = jnp.concatenate([_pair_heads_rows(wo[0:512]), wo[512:1024]], axis=0).astype(BF16)
        sink = jnp.broadcast_to(p['a_sink'][i].astype(F32)[:, None], (A_HEADS, LANES))
        even.append(dict(win=win, wuq=wuq, wukv=wukv, wout=wout, sink=sink,
                         gq=row(p['b_q_norm'][i]), gkv=row(p['b_kv_norm'][i])))
    for i in range(DEPTH // 2):
        w = p['w_in_odd'][i]
        win = jnp.concatenate([_pair_heads_cols(w[:, 0:512]), w[:, 512:]], axis=1).astype(BF16)
        gqk = jnp.concatenate([jnp.tile(p['c_q_norm'][i], C_HEADS),
                               jnp.tile(p['c_k_norm'][i], C_KV_HEADS)]).reshape(1, -1).astype(F32)
        wo = p['w_out_odd'][i]
        wout = jnp.concatenate([_pair_heads_rows(wo[0:512]), wo[512:1024]], axis=0).astype(BF16)
        odd.append(dict(win=win, gqk=gqk, wout=wout, lam=p['d_lambda'][i].astype(F32),
                        gsub=row(p['d_subln'][i])))
    L['even'], L['odd'] = even, odd
    seg = jnp.arange(640) // HEAD_DIM
    L['ones'] = (seg[:, None] == seg[None, :]).astype(BF16)
    L['x_wq'] = [p['x_wq'][l].astype(BF16) for l in range(DEPTH)]
    L['x_wkv'] = [p['x_wkv'][l].astype(BF16) for l in range(DEPTH)]
    L['x_wo'] = [p['x_wo'][l].astype(BF16) for l in range(DEPTH)]
    wr = []
    for l in range(DEPTH):
        w = p['router'][l].astype(F32)
        w_hi = w.astype(BF16)
        w_lo = (w - w_hi.astype(F32)).astype(BF16)
        wr.append(jnp.concatenate([w_hi, w_lo, jnp.zeros((D_MODEL, LANES - 2 * N_EXPERTS), BF16)], axis=1))
    L['router'] = wr
    L['norm_final'] = row(p['norm_final'])
    return L


def _tile(n, pref):
    t = min(n, pref)
    assert n % t == 0, (n, t)
    return t


def _trunk(x, mem, p, L):
    B, S, D = x.shape
    n_mem = mem.shape[1]
    T = B * S
    tm_pre = _tile(S, 512)
    tm_post = _tile(S, 512)
    tq = _tile(S, 256)
    cap = EC_CAPACITY * T // N_EXPERTS
    tc = _tile(cap, 1024)
    tf = 512
    tabs = _tables(S, tq, p['rel_bias'])
    xf = x.reshape(T, D).astype(F32)
    memf = mem.reshape(B * n_mem, D).astype(F32)
    tm_mem = _tile(B * n_mem, 256)
    mla_a = ((0, 64), (128, 160))
    mla_b = ((64, 128), (160, 192))
    for layer in range(DEPTH):
        i = layer // 2
        if layer % 2 == 0:
            w = L['even'][i]
            qa, ka, va, qcat, kcat, vb = _pre_even(xf, S, L['norm_mix'][layer], w['win'], w['gq'], w['gkv'],
                                                   w['wuq'], w['wukv'], tabs['tq'], tabs['tk'], tm_pre)
            o1 = _window_attn(qa, ka, va, tabs['a_bias'], w['sink'], B, S)
            o2 = _pair_attn(qcat, kcat, vb, B, S, tq, 256, False, mla_a, mla_b, "latent_attn")
        else:
            w = L['odd'][i]
            qc, kc, vc, qd, kd, vd = _pre_odd(xf, S, L['norm_mix'][layer], w['win'], w['gqk'], L['ones'],
                                              tabs['tax'], tm_pre)
            o1 = _pair_attn(qc, kc, vc, B, S, tq, 128, True, ((0, 64),), ((64, 128),), "axial_attn")
            lambda_init = 0.8 - 0.6 * math.exp(-0.3 * layer)
            o2 = _diff_attn(qd, kd, vd, w['lam'], w['gsub'], tabs['d_bias'], B, S, tq, lambda_init)
        kv = _norm_matmul(memf, L['norm_mem'][layer], L['x_wkv'][layer], tm_mem, BF16)
        x2, hn, aff, aff_tm = _post(xf, o1, o2, w['wout'], L['norm_cross'][layer], L['x_wq'][layer], kv,
                                    L['x_wo'][layer], L['norm_ffn'][layer], L['router'][layer],
                                    S, n_mem, tm_post)
        xf = _moe(x2, hn, aff, aff_tm, p['e_w_gate'], p['e_w_up'], p['e_w_down'], layer, cap, tc, tf)
    out = _final_norm(xf, L['norm_final'], tm_pre)
    return out.reshape(B, S, D)


def kernel(x_prompt, x_sample, mem_prompt, mem_sample, rel_bias, norm_mix, norm_cross, norm_mem, norm_ffn,
           norm_final, w_in_even, a_sink, b_q_norm, b_kv_norm, b_w_uq, b_w_ukv, w_out_even, w_in_odd,
           c_q_norm, c_k_norm, d_lambda, d_subln, w_out_odd, x_wq, x_wkv, x_wo, router, e_w_gate,
           e_w_up, e_w_down):
    p = dict(rel_bias=rel_bias, norm_mix=norm_mix, norm_cross=norm_cross, norm_mem=norm_mem,
             norm_ffn=norm_ffn, norm_final=norm_final, w_in_even=w_in_even, a_sink=a_sink,
             b_q_norm=b_q_norm, b_kv_norm=b_kv_norm, b_w_uq=b_w_uq, b_w_ukv=b_w_ukv,
             w_out_even=w_out_even, w_in_odd=w_in_odd, c_q_norm=c_q_norm, c_k_norm=c_k_norm,
             d_lambda=d_lambda, d_subln=d_subln, w_out_odd=w_out_odd, x_wq=x_wq, x_wkv=x_wkv,
             x_wo=x_wo, router=router, e_w_gate=e_w_gate, e_w_up=e_w_up, e_w_down=e_w_down)
    L = _prep_weights(p)
    y_prompt = _trunk(x_prompt, mem_prompt, p, L)
    y_sample = _trunk(x_sample, mem_sample, p, L)
    return (y_prompt, y_sample)
```

```python
import functools
import math

import jax
import jax.numpy as jnp
from jax import lax
from jax.experimental import pallas as pl
from jax.experimental.pallas import tpu as pltpu

F32 = jnp.float32
BF16 = jnp.bfloat16

D_MODEL = 1024
DEPTH = 4
HEAD_DIM = 64
QB = 128
WINDOW = 128
A_HEADS = 8
A_KV_HEADS = 2
B_HEADS = 8
B_Q_LORA = 384
B_KV_LORA = 256
B_NOPE = 64
B_ROPE = 32
B_V = 64
C_HEADS = 8
C_KV_HEADS = 2
D_HEADS = 4
X_HEADS = 4
X_HEAD_DIM = D_MODEL // X_HEADS
N_BUCKETS = 32
MAX_DISTANCE = 128
N_EXPERTS = 16
EXPERT_FF = 2 * D_MODEL
EC_CAPACITY = 2
GRID_W = 64
ROPE_BASE = 10000.0
NEG_INF = -1e30
EPS = 1e-6
LOG2E = math.log2(math.e)

LANES = 128
VMEM_LIMIT = 48 * 1024 * 1024

_NT = (((1,), (1,)), ((), ()))


def _cparams():
    return pltpu.CompilerParams(vmem_limit_bytes=VMEM_LIMIT)


def _rms(x, g):
    ms = jnp.mean(x * x, axis=-1, keepdims=True)
    return x * lax.rsqrt(ms + EPS) * g


def _rope(x, tab_ref):
    w = x.shape[1]
    return (x * tab_ref[0]
            + pltpu.roll(x, w - 16, 1) * tab_ref[1]
            + pltpu.roll(x, 16, 1) * tab_ref[2])


def _softmax_rows(s):
    m = jnp.max(s, axis=-1, keepdims=True)
    e = jnp.exp(s - m)
    return e, jnp.sum(e, axis=-1, keepdims=True)


def _pre_even_kernel(x_ref, g_ref, win_ref, gq_ref, gkv_ref, wuq_ref, wukv_ref, tq_ref, tk_ref,
                     qa_ref, ka_ref, va_ref, qcat_ref, kcat_ref, vb_ref):
    hn = _rms(x_ref[...], g_ref[...]).astype(BF16)
    z = jnp.dot(hn, win_ref[...], preferred_element_type=F32)
    qa_ref[...] = (z[:, 0:512] * (HEAD_DIM ** -0.5)).astype(BF16)
    ka_ref[...] = z[:, 512:640].astype(BF16)
    va_ref[...] = z[:, 640:768].astype(BF16)
    cq = _rms(z[:, 768:1152], gq_ref[...]).astype(BF16)
    q = jnp.dot(cq, wuq_ref[...], preferred_element_type=F32)
    for m in range(4):
        qcat_ref[:, 256 * m:256 * (m + 1)] = _rope(q[:, 256 * m:256 * (m + 1)], tq_ref).astype(BF16)
    ckv = _rms(z[:, 1152:1408], gkv_ref[...]).astype(BF16)
    kv = jnp.dot(ckv, wukv_ref[...], preferred_element_type=F32)
    kr = _rope(z[:, 1408:1536], tk_ref).astype(BF16)
    for m in range(4):
        kcat_ref[:, 256 * m:256 * m + 128] = kv[:, 128 * m:128 * (m + 1)].astype(BF16)
        kcat_ref[:, 256 * m + 128:256 * (m + 1)] = kr
    vb_ref[...] = kv[:, 512:1024].astype(BF16)


def _pre_odd_kernel(x_ref, g_ref, win_ref, gqk_ref, ones_ref, tab_ref,
                    qc_ref, kc_ref, vc_ref, qd_ref, kd_ref, vd_ref):
    hn = _rms(x_ref[...], g_ref[...]).astype(BF16)
    z = jnp.dot(hn, win_ref[...], preferred_element_type=F32)
    qk = z[:, 0:640]
    ss = qk * qk
    hi = ss.astype(BF16)
    lo = (ss - hi.astype(F32)).astype(BF16)
    ms = (jnp.dot(hi, ones_ref[...], preferred_element_type=F32)
          + jnp.dot(lo, ones_ref[...], preferred_element_type=F32)) * (1.0 / HEAD_DIM)
    qkn = qk * lax.rsqrt(ms + EPS) * gqk_ref[...]
    for m in range(4):
        blk = _rope(qkn[:, 128 * m:128 * (m + 1)], tab_ref)
        qc_ref[:, 128 * m:128 * (m + 1)] = (blk * (LOG2E * HEAD_DIM ** -0.5)).astype(BF16)
    kc_ref[...] = _rope(qkn[:, 512:640], tab_ref).astype(BF16)
    vc_ref[...] = z[:, 640:768].astype(BF16)
    qd_ref[...] = (z[:, 768:1280] * (LOG2E * HEAD_DIM ** -0.5)).astype(BF16)
    kd_ref[...] = z[:, 1280:1792].astype(BF16)
    vd_ref[...] = z[:, 1792:2304].astype(BF16)


def _const_spec(shape):
    nd = len(shape)
    return pl.BlockSpec(shape, lambda *_: (0,) * nd)


def _pre_even(x, S, g, win, gq, gkv, wuq, wukv, tq, tk, tm):
    T = x.shape[0]
    ns = S // tm
    row = lambda w: pl.BlockSpec((tm, w), lambda i: (i, 0))
    tab = lambda w: pl.BlockSpec((3, tm, w), lambda i: (0, i % ns, 0))
    out_w = (512, 128, 128, 1024, 1024, 512)
    return pl.pallas_call(
        _pre_even_kernel,
        grid=(T // tm,),
        in_specs=[row(D_MODEL), _const_spec(g.shape), _const_spec(win.shape), _const_spec(gq.shape),
                  _const_spec(gkv.shape), _const_spec(wuq.shape), _const_spec(wukv.shape),
                  tab(256), tab(128)],
        out_specs=[row(w) for w in out_w],
        out_shape=[jax.ShapeDtypeStruct((T, w), BF16) for w in out_w],
        compiler_params=_cparams(),
        name="pre_even",
    )(x, g, win, gq, gkv, wuq, wukv, tq, tk)


def _pre_odd(x, S, g, win, gqk, ones, tab3, tm):
    T = x.shape[0]
    ns = S // tm
    row = lambda w: pl.BlockSpec((tm, w), lambda i: (i, 0))
    out_w = (512, 128, 128, 512, 512, 512)
    return pl.pallas_call(
        _pre_odd_kernel,
        grid=(T // tm,),
        in_specs=[row(D_MODEL), _const_spec(g.shape), _const_spec(win.shape), _const_spec(gqk.shape),
                  _const_spec(ones.shape), pl.BlockSpec((3, tm, 128), lambda i: (0, i % ns, 0))],
        out_specs=[row(w) for w in out_w],
        out_shape=[jax.ShapeDtypeStruct((T, w), BF16) for w in out_w],
        compiler_params=_cparams(),
        name="pre_odd",
    )(x, g, win, gqk, ones, tab3)


def _window_kernel(q_ref, kp_ref, kc_ref, kn_ref, vp_ref, vc_ref, vn_ref, bias_ref, sink_ref, o_ref):
    i = pl.program_id(1)
    nb = pl.num_programs(1)
    q = q_ref[...]
    tw = q.shape[0]
    kw = jnp.concatenate([kp_ref[...], kc_ref[...], kn_ref[...]], axis=0)
    vw = jnp.concatenate([vp_ref[...], vc_ref[...], vn_ref[...]], axis=0)
    col = lax.broadcasted_iota(jnp.int32, (1, tw + 2 * QB), 1)
    outside = ((col < QB) & (i == 0)) | ((col >= QB + tw) & (i == nb - 1))
    lane = lax.broadcasted_iota(jnp.int32, (1, LANES), 1)
    for m in range(4):
        qblk = q[:, 128 * m:128 * (m + 1)]
        outs = []
        for half in range(2):
            h = m + 4 * half
            keep = (lane < HEAD_DIM) if half == 0 else (lane >= HEAD_DIM)
            qm = jnp.where(keep, qblk, jnp.zeros_like(qblk))
            s = lax.dot_general(qm, kw, _NT, preferred_element_type=F32) + bias_ref[h]
            s = jnp.where(outside, NEG_INF, s)
            sk = sink_ref[h:h + 1, 0:1]
            mx = jnp.maximum(jnp.max(s, axis=-1, keepdims=True), sk)
            e = jnp.exp(s - mx)
            den = jnp.sum(e, axis=-1, keepdims=True) + jnp.exp(sk - mx)
            outs.append(jnp.dot(e.astype(BF16), vw, preferred_element_type=F32) / den)
        o_ref[:, 128 * m:128 * (m + 1)] = jnp.where(lane < HEAD_DIM, outs[0], outs[1]).astype(BF16)


def _window_attn(qa, ka, va, bias, sink, B, S, tw):
    T = qa.shape[0]
    nb = S // tw
    r = tw // QB
    nqb = S // QB
    side = lambda d: pl.BlockSpec(
        (QB, 128), lambda b, i: (b * nqb + jnp.clip(r * i + d, 0, nqb - 1), 0))
    mid = pl.BlockSpec((tw, 128), lambda b, i: (b * nb + i, 0))
    return pl.pallas_call(
        _window_kernel,
        grid=(B, nb),
        in_specs=[pl.BlockSpec((tw, 512), lambda b, i: (b * nb + i, 0)),
                  side(-1), mid, side(r), side(-1), mid, side(r),
                  _const_spec(bias.shape), _const_spec(sink.shape)],
        out_specs=pl.BlockSpec((tw, 512), lambda b, i: (b * nb + i, 0)),
        out_shape=jax.ShapeDtypeStruct((T, 512), BF16),
        compiler_params=_cparams(),
        name="window_attn",
    )(qa, ka, ka, ka, va, va, va, bias, sink)


def _pair_kernel(q_ref, k_ref, v_ref, o_ref, vx_ref, *, lanes_a, lanes_b, kc):
    @pl.when(pl.program_id(2) == 0)
    def _():
        vx_ref[:, 0:LANES] = v_ref[...]
        vx_ref[:, LANES:2 * LANES] = jnp.ones((vx_ref.shape[0], LANES), BF16)

    q = q_ref[...]
    tq, wq = q.shape
    S = k_ref.shape[0]
    lane = lax.broadcasted_iota(jnp.int32, (1, wq), 1)
    qms = []
    for ranges in (lanes_a, lanes_b):
        keep = functools.reduce(jnp.logical_or, [(lane >= lo) & (lane < hi) for lo, hi in ranges])
        qms.append(jnp.where(keep, q, jnp.zeros_like(q)))
    m = [jnp.full((tq, 1), NEG_INF, F32) for _ in range(2)]
    acc = [jnp.zeros((tq, 2 * LANES), F32) for _ in range(2)]
    for c in range(S // kc):
        kchunk = k_ref[kc * c:kc * (c + 1), :]
        vchunk = vx_ref[kc * c:kc * (c + 1), :]
        for h in range(2):
            s = lax.dot_general(qms[h], kchunk, _NT, preferred_element_type=F32)
            m_new = jnp.maximum(m[h], jnp.max(s, axis=-1, keepdims=True))
            p = jnp.exp2(s - m_new).astype(BF16)
            acc[h] = jnp.exp2(m[h] - m_new) * acc[h] + jnp.dot(p, vchunk, preferred_element_type=F32)
            m[h] = m_new
    outs = [a[:, 0:LANES] / a[:, LANES:LANES + 1] for a in acc]
    lane_o = lax.broadcasted_iota(jnp.int32, (1, LANES), 1)
    o_ref[...] = jnp.where(lane_o < 64, outs[0], outs[1]).astype(BF16)


def _pair_attn(q, k, v, B, S, tq, wq, shared_kv, lanes_a, lanes_b, name):
    T = q.shape[0]
    nq = S // tq
    kc = _tile(S, 512)
    kv_map = (lambda b, m, i: (b, 0)) if shared_kv else (lambda b, m, i: (b, m))
    return pl.pallas_call(
        functools.partial(_pair_kernel, lanes_a=lanes_a, lanes_b=lanes_b, kc=kc),
        scratch_shapes=[pltpu.VMEM((S, 2 * LANES), BF16)],
        grid=(B, 4, nq),
        in_specs=[pl.BlockSpec((tq, wq), lambda b, m, i: (b * nq + i, m)),
                  pl.BlockSpec((S, wq), kv_map),
                  pl.BlockSpec((S, 128), kv_map)],
        out_specs=pl.BlockSpec((tq, 128), lambda b, m, i: (b * nq + i, m)),
        out_shape=jax.ShapeDtypeStruct((T, 512), BF16),
        compiler_params=_cparams(),
        name=name,
    )(q, k, v)


def _diff_kernel(lam_ref, gsub_ref, q_ref, k_ref, v_ref, bias_ref, o_ref, vx_ref, *, S, tq, kc, lambda_init):
    i = pl.program_id(2)

    @pl.when(i == 0)
    def _():
        vx_ref[:, 0:LANES] = v_ref[...]
        vx_ref[:, LANES:2 * LANES] = jnp.ones((vx_ref.shape[0], LANES), BF16)

    r = tq // QB
    nb = kc // QB
    q = q_ref[...]
    lp = lam_ref[...]
    lam = (jnp.exp(jnp.sum(lp[0:1] * lp[1:2], axis=-1, keepdims=True))
           - jnp.exp(jnp.sum(lp[2:3] * lp[3:4], axis=-1, keepdims=True)) + lambda_init)
    lane = lax.broadcasted_iota(jnp.int32, (1, LANES), 1)
    qms = [jnp.where(lane < HEAD_DIM, q, jnp.zeros_like(q)), jnp.where(lane >= HEAD_DIM, q, jnp.zeros_like(q))]
    m = [jnp.full((tq, 1), NEG_INF, F32) for _ in range(2)]
    acc = [jnp.zeros((tq, 2 * LANES), F32) for _ in range(2)]
    for c in range(S // kc):
        kchunk = k_ref[kc * c:kc * (c + 1), :]
        vchunk = vx_ref[kc * c:kc * (c + 1), :]
        bias = jnp.concatenate(
            [bias_ref[0, jnp.clip(nb * c + j - r * i + 2, 0, r + 3)] for j in range(nb)], axis=1)
        for h in range(2):
            s = lax.dot_general(qms[h], kchunk, _NT, preferred_element_type=F32) + bias
            m_new = jnp.maximum(m[h], jnp.max(s, axis=-1, keepdims=True))
            p = jnp.exp2(s - m_new).astype(BF16)
            acc[h] = jnp.exp2(m[h] - m_new) * acc[h] + jnp.dot(p, vchunk, preferred_element_type=F32)
            m[h] = m_new
    outs = [a[:, 0:LANES] / a[:, LANES:LANES + 1] for a in acc]
    o = outs[0] - lam * outs[1]
    o_ref[...] = (_rms(o, gsub_ref[...]) * (1.0 - lambda_init)).astype(BF16)


def _diff_attn(qd, kd, vd, lam_p, gsub, bias, B, S, tq, lambda_init):
    T = qd.shape[0]
    nq = S // tq
    nt = bias.shape[1]
    kc = _tile(S, 512)
    return pl.pallas_call(
        functools.partial(_diff_kernel, S=S, tq=tq, kc=kc, lambda_init=lambda_init),
        grid=(B, D_HEADS, nq),
        in_specs=[_const_spec(lam_p.shape), _const_spec(gsub.shape),
                  pl.BlockSpec((tq, 128), lambda b, h, i: (b * nq + i, h)),
                  pl.BlockSpec((S, 128), lambda b, h, i: (b, h)),
                  pl.BlockSpec((S, 128), lambda b, h, i: (b, h)),
                  pl.BlockSpec((1, nt, tq, 128), lambda b, h, i: (h, 0, 0, 0))],
        out_specs=pl.BlockSpec((tq, 128), lambda b, h, i: (b * nq + i, h)),
        out_shape=jax.ShapeDtypeStruct((T, 512), BF16),
        scratch_shapes=[pltpu.VMEM((S, 2 * LANES), BF16)],
        compiler_params=_cparams(),
        name="diff_attn",
    )(lam_p, gsub, qd, kd, vd, bias)


def _norm_matmul_kernel(x_ref, g_ref, w_ref, o_ref):
    hn = _rms(x_ref[...], g_ref[...]).astype(BF16)
    o_ref[...] = jnp.dot(hn, w_ref[...], preferred_element_type=F32).astype(o_ref.dtype)


def _norm_matmul(x, g, w, tm, out_dtype):
    T, K = x.shape
    N = w.shape[1]
    return pl.pallas_call(
        _norm_matmul_kernel,
        grid=(T // tm,),
        in_specs=[pl.BlockSpec((tm, K), lambda i: (i, 0)), _const_spec(g.shape), _const_spec(w.shape)],
        out_specs=pl.BlockSpec((tm, N), lambda i: (i, 0)),
        out_shape=jax.ShapeDtypeStruct((T, N), out_dtype),
        compiler_params=_cparams(),
        name="norm_matmul",
    )(x, g, w)


def _post_kernel(x_ref, oa_ref, ob_ref, wout_ref, gx_ref, wq_ref, kv_ref, wo_ref, gf_ref, wr_ref,
                 x2_ref, hn_ref, aff_ref, afft_ref):
    x1 = (x_ref[...]
          + jnp.dot(oa_ref[...], wout_ref[0:512, :], preferred_element_type=F32)
          + jnp.dot(ob_ref[...], wout_ref[512:1024, :], preferred_element_type=F32))
    hc = _rms(x1, gx_ref[...]).astype(BF16)
    q = (jnp.dot(hc, wq_ref[...], preferred_element_type=F32) * (X_HEAD_DIM ** -0.5)).astype(BF16)
    heads = []
    for h in range(X_HEADS):
        lo = X_HEAD_DIM * h
        kh = kv_ref[:, lo:lo + X_HEAD_DIM]
        vh = kv_ref[:, D_MODEL + lo:D_MODEL + lo + X_HEAD_DIM]
        s = lax.dot_general(q[:, lo:lo + X_HEAD_DIM], kh, _NT, preferred_element_type=F32)
        e, l = _softmax_rows(s)
        heads.append((jnp.dot(e.astype(BF16), vh, preferred_element_type=F32) / l).astype(BF16))
    o = jnp.concatenate(heads, axis=1)
    x2 = x1 + jnp.dot(o, wo_ref[...], preferred_element_type=F32)
    x2_ref[...] = x2
    hf = _rms(x2, gf_ref[...])
    hn_ref[...] = hf.astype(BF16)
    tm = hf.shape[0]
    h_hi = hf.astype(BF16)
    h_lo = (hf - h_hi.astype(F32)).astype(BF16)
    r = jnp.dot(jnp.concatenate([h_hi, h_lo], axis=0), wr_ref[...], preferred_element_type=F32)
    top = r[0:tm]
    logits = top + pltpu.roll(top, LANES - N_EXPERTS, 1) + r[tm:2 * tm]
    valid = lax.broadcasted_iota(jnp.int32, (1, LANES), 1) < N_EXPERTS
    mx = jnp.max(jnp.where(valid, logits, NEG_INF), axis=-1, keepdims=True)
    e = jnp.where(valid, jnp.exp(logits - mx), 0.0)
    aff = e / jnp.sum(e, axis=-1, keepdims=True)
    afft_ref[...] = aff[:, 0:N_EXPERTS]
    aff_t = aff.T
    for c in range(tm // LANES):
        aff_ref[c] = aff_t[0:N_EXPERTS, LANES * c:LANES * (c + 1)]


def _post(x, oa, ob, wout, gx, wq, kv, wo, gf, wr3, S, n_mem, tm):
    T = x.shape[0]
    per_b = S // tm
    row = lambda w: pl.BlockSpec((tm, w), lambda i: (i, 0))
    return pl.pallas_call(
        _post_kernel,
        grid=(T // tm,),
        in_specs=[row(D_MODEL), row(512), row(512), _const_spec(wout.shape), _const_spec(gx.shape),
                  _const_spec(wq.shape), pl.BlockSpec((n_mem, 2 * D_MODEL), lambda i: (i // per_b, 0)),
                  _const_spec(wo.shape), _const_spec(gf.shape), _const_spec(wr3.shape)],
        out_specs=[row(D_MODEL), row(D_MODEL),
                   pl.BlockSpec((tm // LANES, N_EXPERTS, LANES), lambda i: (i, 0, 0)), row(N_EXPERTS)],
        out_shape=[jax.ShapeDtypeStruct((T, D_MODEL), F32), jax.ShapeDtypeStruct((T, D_MODEL), BF16),
                   jax.ShapeDtypeStruct((T // LANES, N_EXPERTS, LANES), F32),
                   jax.ShapeDtypeStruct((T, N_EXPERTS), F32)],
        compiler_params=_cparams(),
        name="post",
    )(x, oa, ob, wout, gx, wq, kv, wo, gf, wr3)


def _route_kernel(aff_ref, tri_ref, ones_ref, carrymat_ref, key_ref, carry_ref, *, cap):
    x = aff_ref[...]
    nj = x.shape[0]

    def count(mask):
        c = jnp.sum(mask.astype(F32), axis=0, keepdims=True)
        return jnp.sum(c, axis=2, keepdims=True)

    def bisect(b, thr):
        cand = thr | lax.shift_left(jnp.int32(1), 30 - b)
        return jnp.where(count(x >= pltpu.bitcast(cand, F32)) >= cap, cand, thr)

    thr = pltpu.bitcast(lax.fori_loop(0, 31, bisect, jnp.zeros((1, N_EXPERTS, 1), jnp.int32)), F32)
    gt = x > thr
    eq = x == thr
    need = cap - count(gt)

    def prefix(mask):
        v = mask.astype(BF16).reshape(nj * N_EXPERTS, LANES)
        within = jnp.dot(v, tri_ref[...], preferred_element_type=F32)
        tot = jnp.dot(v, ones_ref[...], preferred_element_type=F32)
        carry = jnp.dot(carrymat_ref[...], tot.astype(BF16), preferred_element_type=F32)
        return within + carry, carry

    eq_rank, _ = prefix(eq)
    sel = gt | (eq & (eq_rank.reshape(nj, N_EXPERTS, LANES) < need))
    rank, carry = prefix(sel)
    key_ref[...] = jnp.where(sel, rank.reshape(nj, N_EXPERTS, LANES), -1.0)
    carry_ref[...] = carry


def _route(aff3, tri, ones, carrymat, cap):
    nj = aff3.shape[0]
    rows = nj * N_EXPERTS
    return pl.pallas_call(
        functools.partial(_route_kernel, cap=cap),
        out_shape=[jax.ShapeDtypeStruct((nj, N_EXPERTS, LANES), F32),
                   jax.ShapeDtypeStruct((rows, LANES), F32)],
        compiler_params=_cparams(),
        name="route",
    )(aff3, tri, ones, carrymat)


def _window_start(c0, cap, win):
    a = jnp.minimum((c0 // 16) * 16, cap - win)
    return pl.multiple_of(a, 16)


GATHER_EXPERTS = 2


def _gather_kernel(c0_ref, key_ref, hn_ref, xe_ref, *, cap, win, nq):
    sb = pl.program_id(1)

    @pl.when(sb == 0)
    def _():
        xe_ref[...] = jnp.zeros_like(xe_ref)

    slot = lax.broadcasted_iota(jnp.int32, (win, 1), 0).astype(F32)

    def add_window(j, q, a, lo):
        keyrow = jnp.concatenate([key_ref[j, 2 * q:2 * q + 1, :], key_ref[j, 2 * q + 1:2 * q + 2, :]], axis=1)
        hit = (keyrow - a.astype(F32) == slot) & (keyrow >= lo.astype(F32))
        rows = jnp.dot(hit.astype(BF16), hn_ref[256 * q:256 * (q + 1), :],
                       preferred_element_type=F32)
        xe_ref[j, pl.ds(a, win), :] = xe_ref[j, pl.ds(a, win), :] + rows.astype(BF16)

    experts = [pl.program_id(0) * GATHER_EXPERTS + j for j in range(GATHER_EXPERTS)]
    starts = [[_window_start(c0_ref[e, sb * nq + q], cap, win) for q in range(nq)] for e in experts]
    for q in range(nq):
        for j, e in enumerate(experts):
            add_window(j, q, starts[j][q], c0_ref[e, sb * nq + q])
    for q in range(nq):
        for j, e in enumerate(experts):
            nxt = starts[j][q] + win
            pl.when(c0_ref[e, sb * nq + q + 1] > nxt)(
                functools.partial(add_window, j, q, _window_start(nxt, cap, win), nxt))


def _gather(c0, key_em, hn, cap, win, tsb):
    T, D = hn.shape
    nq = tsb // 256
    ge = GATHER_EXPERTS
    grid_spec = pltpu.PrefetchScalarGridSpec(
        num_scalar_prefetch=1,
        grid=(N_EXPERTS // ge, T // tsb),
        in_specs=[pl.BlockSpec((ge, tsb // LANES, LANES), lambda e, sb, c0: (e, sb, 0)),
                  pl.BlockSpec((tsb, D), lambda e, sb, c0: (sb, 0))],
        out_specs=pl.BlockSpec((ge, cap, D), lambda e, sb, c0: (e, 0, 0)),
    )
    return pl.pallas_call(
        functools.partial(_gather_kernel, cap=cap, win=win, nq=nq),
        grid_spec=grid_spec,
        out_shape=jax.ShapeDtypeStruct((N_EXPERTS, cap, D), BF16),
        compiler_params=_cparams(),
        name="moe_gather",
    )(c0, key_em, hn)


def _ffn_kernel(xe_ref, wg_ref, wu_ref, wd_ref, y_ref, acc_ref):
    f = pl.program_id(2)

    @pl.when(f == 0)
    def _():
        acc_ref[...] = jnp.zeros_like(acc_ref)

    xe = xe_ref[0]
    a = jnp.dot(xe, wg_ref[0, 0].astype(BF16), preferred_element_type=F32)
    u = jnp.dot(xe, wu_ref[0, 0].astype(BF16), preferred_element_type=F32)
    h = (a * (1.0 / (1.0 + jnp.exp(-a))) * u).astype(BF16)
    acc_ref[...] += jnp.dot(h, wd_ref[0, 0].astype(BF16), preferred_element_type=F32)

    @pl.when(f == pl.num_programs(2) - 1)
    def _():
        y_ref[0] = acc_ref[...].astype(BF16)


def _expert_ffn(xe, wg, wu, wd, layer, tc, tf):
    E, cap, D = xe.shape
    F = wg.shape[3]
    return pl.pallas_call(
        _ffn_kernel,
        grid=(E, cap // tc, F // tf),
        in_specs=[pl.BlockSpec((1, tc, D), lambda e, c, f: (e, c, 0)),
                  pl.BlockSpec((1, 1, D, tf), lambda e, c, f: (layer, e, 0, f)),
                  pl.BlockSpec((1, 1, D, tf), lambda e, c, f: (layer, e, 0, f)),
                  pl.BlockSpec((1, 1, tf, D), lambda e, c, f: (layer, e, f, 0))],
        out_specs=pl.BlockSpec((1, tc, D), lambda e, c, f: (e, c, 0)),
        out_shape=jax.ShapeDtypeStruct((E, cap, D), BF16),
        scratch_shapes=[pltpu.VMEM((tc, D), F32)],
        compiler_params=_cparams(),
        name="expert_ffn",
    )(xe, wg, wu, wd)


def _combine_kernel(c0_ref, x_ref, key_ref, aff_ref, y_ref, o_ref, *, cap, win, nq):
    sb = pl.program_id(0)
    e = pl.program_id(1)

    @pl.when(e == 0)
    def _():
        o_ref[...] = x_ref[...]

    mine = lax.broadcasted_iota(jnp.int32, (1, N_EXPERTS), 1) == e
    keycol = jnp.sum(jnp.where(mine, key_ref[...], 0.0), axis=-1, keepdims=True)
    gate = jnp.sum(jnp.where(mine, aff_ref[...], 0.0), axis=-1, keepdims=True)
    slot = lax.broadcasted_iota(jnp.int32, (1, win), 1).astype(F32)

    def add_window(q, a, lo):
        kq = keycol[256 * q:256 * (q + 1)]
        hit = (kq - a.astype(F32) == slot) & (kq >= lo.astype(F32))
        rows = jnp.dot(hit.astype(BF16), y_ref[0, pl.ds(a, win), :], preferred_element_type=F32)
        o_ref[256 * q:256 * (q + 1), :] = (o_ref[256 * q:256 * (q + 1), :]
                                           + rows * gate[256 * q:256 * (q + 1)])

    starts = [_window_start(c0_ref[e, sb * nq + q], cap, win) for q in range(nq)]
    for q in range(nq):
        add_window(q, starts[q], c0_ref[e, sb * nq + q])
    for q in range(nq):
        nxt = starts[q] + win
        pl.when(c0_ref[e, sb * nq + q + 1] > nxt)(
            functools.partial(add_window, q, _window_start(nxt, cap, win), nxt))


def _combine(c0, x, key_tm, aff_tm, y, cap, win, tsb):
    T, D = x.shape
    nq = tsb // 256
    grid_spec = pltpu.PrefetchScalarGridSpec(
        num_scalar_prefetch=1,
        grid=(T // tsb, N_EXPERTS),
        in_specs=[pl.BlockSpec((tsb, D), lambda sb, e, c0: (sb, 0)),
                  pl.BlockSpec((tsb, N_EXPERTS), lambda sb, e, c0: (sb, 0)),
                  pl.BlockSpec((tsb, N_EXPERTS), lambda sb, e, c0: (sb, 0)),
                  pl.BlockSpec((1, cap, D), lambda sb, e, c0: (e, 0, 0))],
        out_specs=pl.BlockSpec((tsb, D), lambda sb, e, c0: (sb, 0)),
    )
    return pl.pallas_call(
        functools.partial(_combine_kernel, cap=cap, win=win, nq=nq),
        grid_spec=grid_spec,
        out_shape=jax.ShapeDtypeStruct((T, D), F32),
        compiler_params=_cparams(),
        name="moe_combine",
    )(c0, x, key_tm, aff_tm, y)


def _route_constants(nj):
    lane = jnp.arange(LANES)
    tri = (lane[:, None] < lane[None, :]).astype(BF16)
    ones = jnp.ones((LANES, LANES), BF16)
    r = jnp.arange(nj * N_EXPERTS)
    carrymat = ((r[None, :] % N_EXPERTS == r[:, None] % N_EXPERTS) & (r[None, :] < r[:, None])).astype(BF16)
    return tri, ones, carrymat


def _moe(x2, hn, aff3, aff_tm, wg, wu, wd, layer, cap, tc, tf):
    T, D = x2.shape
    nj = T // LANES
    win = min(256, cap)
    tsb = min(2048, T)
    key3, carry = _route(aff3, *_route_constants(nj), cap)
    key_em = key3.transpose(1, 0, 2)
    key_tm = key3.transpose(0, 2, 1).reshape(T, N_EXPERTS)
    starts = carry[:, 0].reshape(nj, N_EXPERTS)[::2].T.astype(jnp.int32)
    c0 = jnp.concatenate([starts, jnp.full((N_EXPERTS, 1), cap, jnp.int32)], axis=1)
    xe = _gather(c0, key_em, hn, cap, win, tsb)
    y = _expert_ffn(xe, wg, wu, wd, layer, tc, tf)
    return _combine(c0, x2, key_tm, aff_tm, y, cap, win, tsb)


def _final_norm_kernel(x_ref, g_ref, o_ref):
    o_ref[...] = _rms(x_ref[...], g_ref[...])


def _final_norm(x, g, tm):
    T, D = x.shape
    return pl.pallas_call(
        _final_norm_kernel,
        grid=(T // tm,),
        in_specs=[pl.BlockSpec((tm, D), lambda i: (i, 0)), _const_spec(g.shape)],
        out_specs=pl.BlockSpec((tm, D), lambda i: (i, 0)),
        out_shape=jax.ShapeDtypeStruct((T, D), F32),
        name="final_norm",
    )(x, g)


def _t5_bucket(rel):
    half = N_BUCKETS // 2
    exact = half // 2
    n = jnp.abs(rel)
    big = exact + (jnp.log(jnp.maximum(n, 1).astype(F32) / exact)
                   / math.log(MAX_DISTANCE / exact) * (half - exact)).astype(jnp.int32)
    big = jnp.minimum(big, half - 1)
    return jnp.where(rel > 0, half, 0) + jnp.where(n < exact, n, big)


def _rope_freqs(pos, dim):
    freqs = ROPE_BASE ** (-jnp.arange(0, dim, 2, dtype=F32) / dim)
    ang = pos.astype(F32)[:, None] * freqs[None, :]
    return jnp.cos(ang), jnp.sin(ang)


def _rope_lane_table(cos, sin, active):
    w = cos.shape[1]
    first = (jnp.arange(w) % 32) < 16
    c = jnp.where(active[None, :], cos, 1.0)
    s1 = jnp.where((active & first)[None, :], -sin, 0.0)
    s2 = jnp.where((active & ~first)[None, :], sin, 0.0)
    return jnp.stack([c, s1, s2]).astype(F32)


def _pair_heads_cols(w):
    k = w.shape[0]
    return w.reshape(k, 2, 4, HEAD_DIM).transpose(0, 2, 1, 3).reshape(k, 8 * HEAD_DIM)


def _pair_heads_rows(w):
    n = w.shape[1]
    return w.reshape(2, 4, HEAD_DIM, n).transpose(1, 0, 2, 3).reshape(8 * HEAD_DIM, n)


def _bias_lookup(table, bucket):
    shape = (table.shape[1],) + (1,) * bucket.ndim
    out = jnp.zeros((table.shape[1],) + bucket.shape, F32)
    for b in range(N_BUCKETS):
        out = jnp.where(bucket[None] == b, table[b].reshape(shape), out)
    return out


def _tables(S, tq_diff, tw, rel_bias):
    t = jnp.arange(S)
    lane128 = jnp.arange(128)
    lane256 = jnp.arange(256)
    cos_t, sin_t = _rope_freqs(t, B_ROPE)
    scale_b = LOG2E * (B_NOPE + B_ROPE) ** -0.5
    tq = _rope_lane_table(jnp.tile(cos_t, (1, 16)), jnp.tile(sin_t, (1, 16)),
                          (lane256 >= 128) & (lane256 < 192)) * scale_b
    tk = _rope_lane_table(jnp.tile(cos_t, (1, 8)), jnp.tile(sin_t, (1, 8)), lane128 < 64)
    cos_r, sin_r = _rope_freqs(t // GRID_W, HEAD_DIM // 2)
    cos_c, sin_c = _rope_freqs(t % GRID_W, HEAD_DIM // 2)
    cos_ax = jnp.tile(jnp.concatenate([cos_r, cos_r, cos_c, cos_c], axis=1), (1, 2))
    sin_ax = jnp.tile(jnp.concatenate([sin_r, sin_r, sin_c, sin_c], axis=1), (1, 2))
    tax = _rope_lane_table(cos_ax, sin_ax, jnp.ones((128,), bool))
    table = rel_bias.astype(F32)
    rel_win = jnp.arange(tw + 2 * QB)[None, :] - QB - jnp.arange(tw)[:, None]
    a_bias = _bias_lookup(table[:, :A_HEADS], _t5_bucket(rel_win))
    a_bias = jnp.where((jnp.abs(rel_win) <= WINDOW)[None], a_bias, NEG_INF)
    r = tq_diff // QB
    rel_d = (QB * (jnp.arange(r + 4)[:, None, None] - 2)
             + jnp.arange(QB)[None, None, :] - jnp.arange(tq_diff)[None, :, None])
    d_bias = _bias_lookup(table[:, A_HEADS:], _t5_bucket(rel_d)) * LOG2E
    return dict(tq=tq, tk=tk, tax=tax, a_bias=a_bias, d_bias=d_bias)


def _prep_weights(p):
    row = lambda v: v.reshape(1, -1).astype(F32)
    L = {}
    L['norm_mix'] = [row(p['norm_mix'][l]) for l in range(DEPTH)]
    L['norm_cross'] = [row(p['norm_cross'][l]) for l in range(DEPTH)]
    L['norm_mem'] = [row(p['norm_mem'][l]) for l in range(DEPTH)]
    L['norm_ffn'] = [row(p['norm_ffn'][l]) for l in range(DEPTH)]
    even, odd = [], []
    for i in range((DEPTH + 1) // 2):
        w = p['w_in_even'][i]
        kr = w[:, 1408:1440]
        win = jnp.concatenate([_pair_heads_cols(w[:, 0:512]), w[:, 512:1408], kr, kr,
                               jnp.zeros((D_MODEL, 64), w.dtype)], axis=1).astype(BF16)
        uq = p['b_w_uq'][i].reshape(B_Q_LORA, B_HEADS, B_NOPE + B_ROPE)
        blocks = []
        for m in range(4):
            blocks += [uq[:, 2 * m, :B_NOPE], uq[:, 2 * m + 1, :B_NOPE],
                       uq[:, 2 * m, B_NOPE:], uq[:, 2 * m + 1, B_NOPE:],
                       jnp.zeros((B_Q_LORA, 64), uq.dtype)]
        wuq = jnp.concatenate(blocks, axis=1).astype(BF16)
        ukv = p['b_w_ukv'][i].reshape(B_KV_LORA, B_HEADS, B_NOPE + B_V)
        wukv = jnp.concatenate([ukv[:, :, :B_NOPE].reshape(B_KV_LORA, -1),
                                ukv[:, :, B_NOPE:].reshape(B_KV_LORA, -1)], axis=1).astype(BF16)
        wo = p['w_out_even'][i]
        wout = jnp.concatenate([_pair_heads_rows(wo[0:512]), wo[512:1024]], axis=0).astype(BF16)
        sink = jnp.broadcast_to(p['a_sink'][i].astype(F32)[:, None], (A_HEADS, LANES))
        even.append(dict(win=win, wuq=wuq, wukv=wukv, wout=wout, sink=sink,
                         gq=row(p['b_q_norm'][i]), gkv=row(p['b_kv_norm'][i])))
    for i in range(DEPTH // 2):
        w = p['w_in_odd'][i]
        win = jnp.concatenate([_pair_heads_cols(w[:, 0:512]), w[:, 512:]], axis=1).astype(BF16)
        gqk = jnp.concatenate([jnp.tile(p['c_q_norm'][i], C_HEADS),
                               jnp.tile(p['c_k_norm'][i], C_KV_HEADS)]).reshape(1, -1).astype(F32)
        wo = p['w_out_odd'][i]
        wout = jnp.concatenate([_pair_heads_rows(wo[0:512]), wo[512:1024]], axis=0).astype(BF16)
        odd.append(dict(win=win, gqk=gqk, wout=wout, lam=p['d_lambda'][i].astype(F32),
                        gsub=row(p['d_subln'][i])))
    L['even'], L['odd'] = even, odd
    seg = jnp.arange(640) // HEAD_DIM
    L['ones'] = (seg[:, None] == seg[None, :]).astype(BF16)
    L['x_wq'] = [p['x_wq'][l].astype(BF16) for l in range(DEPTH)]
    L['x_wkv'] = [p['x_wkv'][l].astype(BF16) for l in range(DEPTH)]
    L['x_wo'] = [p['x_wo'][l].astype(BF16) for l in range(DEPTH)]
    wr = []
    for l in range(DEPTH):
        w = p['router'][l].astype(F32)
        w_hi = w.astype(BF16)
        w_lo = (w - w_hi.astype(F32)).astype(BF16)
        wr.append(jnp.concatenate([w_hi, w_lo, jnp.zeros((D_MODEL, LANES - 2 * N_EXPERTS), BF16)], axis=1))
    L['router'] = wr
    L['norm_final'] = row(p['norm_final'])
    return L


def _tile(n, pref):
    t = min(n, pref)
    assert n % t == 0, (n, t)
    return t


def _trunk(x, mem, p, L):
    B, S, D = x.shape
    n_mem = mem.shape[1]
    T = B * S
    tm_pre = _tile(S, 512)
    tm_post = _tile(S, 512)
    tq = _tile(S, 512)
    tq_pair = _tile(S, 1024)
    cap = EC_CAPACITY * T // N_EXPERTS
    tc = _tile(cap, 1024)
    tf = 512
    tw = _tile(S, 256)
    tabs = _tables(S, tq, tw, p['rel_bias'])
    xf = x.reshape(T, D).astype(F32)
    memf = mem.reshape(B * n_mem, D).astype(F32)
    tm_mem = _tile(B * n_mem, 256)
    mla_a = ((0, 64), (128, 160))
    mla_b = ((64, 128), (160, 192))
    for layer in range(DEPTH):
        i = layer // 2
        if layer % 2 == 0:
            w = L['even'][i]
            qa, ka, va, qcat, kcat, vb = _pre_even(xf, S, L['norm_mix'][layer], w['win'], w['gq'], w['gkv'],
                                                   w['wuq'], w['wukv'], tabs['tq'], tabs['tk'], tm_pre)
            o1 = _window_attn(qa, ka, va, tabs['a_bias'], w['sink'], B, S, tw)
            o2 = _pair_attn(qcat, kcat, vb, B, S, tq_pair, 256, False, mla_a, mla_b, "latent_attn")
        else:
            w = L['odd'][i]
            qc, kc, vc, qd, kd, vd = _pre_odd(xf, S, L['norm_mix'][layer], w['win'], w['gqk'], L['ones'],
                                              tabs['tax'], tm_pre)
            o1 = _pair_attn(qc, kc, vc, B, S, tq_pair, 128, True, ((0, 64),), ((64, 128),), "axial_attn")
            lambda_init = 0.8 - 0.6 * math.exp(-0.3 * layer)
            o2 = _diff_attn(qd, kd, vd, w['lam'], w['gsub'], tabs['d_bias'], B, S, tq, lambda_init)
        kv = _norm_matmul(memf, L['norm_mem'][layer], L['x_wkv'][layer], tm_mem, BF16)
        x2, hn, aff, aff_tm = _post(xf, o1, o2, w['wout'], L['norm_cross'][layer], L['x_wq'][layer], kv,
                                    L['x_wo'][layer], L['norm_ffn'][layer], L['router'][layer],
                                    S, n_mem, tm_post)
        xf = _moe(x2, hn, aff, aff_tm, p['e_w_gate'], p['e_w_up'], p['e_w_down'], layer, cap, tc, tf)
    out = _final_norm(xf, L['norm_final'], tm_pre)
    return out.reshape(B, S, D)


def kernel(x_prompt, x_sample, mem_prompt, mem_sample, rel_bias, norm_mix, norm_cross, norm_mem, norm_ffn,
           norm_final, w_in_even, a_sink, b_q_norm, b_kv_norm, b_w_uq, b_w_ukv, w_out_even, w_in_odd,
           c_q_norm, c_k_norm, d_lambda, d_subln, w_out_odd, x_wq, x_wkv, x_wo, router, e_w_gate,
           e_w_up, e_w_down):
    p = dict(rel_bias=rel_bias, norm_mix=norm_mix, norm_cross=norm_cross, norm_mem=norm_mem,
             norm_ffn=norm_ffn, norm_final=norm_final, w_in_even=w_in_even, a_sink=a_sink,
             b_q_norm=b_q_norm, b_kv_norm=b_kv_norm, b_w_uq=b_w_uq, b_w_ukv=b_w_ukv,
             w_out_even=w_out_even, w_in_odd=w_in_odd, c_q_norm=c_q_norm, c_k_norm=c_k_norm,
             d_lambda=d_lambda, d_subln=d_subln, w_out_odd=w_out_odd, x_wq=x_wq, x_wkv=x_wkv,
             x_wo=x_wo, router=router, e_w_gate=e_w_gate, e_w_up=e_w_up, e_w_down=e_w_down)
    L = _prep_weights(p)
    y_prompt = _trunk(x_prompt, mem_prompt, p, L)
    y_sample = _trunk(x_sample, mem_sample, p, L)
    return (y_prompt, y_sample)
```

```python
import functools
import math

import jax
import jax.numpy as jnp
from jax import lax
from jax.experimental import pallas as pl
from jax.experimental.pallas import tpu as pltpu

F32 = jnp.float32
BF16 = jnp.bfloat16

D_MODEL = 1024
DEPTH = 4
HEAD_DIM = 64
QB = 128
WINDOW = 128
A_HEADS = 8
A_KV_HEADS = 2
B_HEADS = 8
B_Q_LORA = 384
B_KV_LORA = 256
B_NOPE = 64
B_ROPE = 32
B_V = 64
C_HEADS = 8
C_KV_HEADS = 2
D_HEADS = 4
X_HEADS = 4
X_HEAD_DIM = D_MODEL // X_HEADS
N_BUCKETS = 32
MAX_DISTANCE = 128
N_EXPERTS = 16
EXPERT_FF = 2 * D_MODEL
EC_CAPACITY = 2
GRID_W = 64
ROPE_BASE = 10000.0
NEG_INF = -1e30
EPS = 1e-6
LOG2E = math.log2(math.e)

LANES = 128
VMEM_LIMIT = 48 * 1024 * 1024
VMEM_LIMIT_BIG = 56 * 1024 * 1024

_NT = (((1,), (1,)), ((), ()))


def _cparams(limit=VMEM_LIMIT):
    return pltpu.CompilerParams(vmem_limit_bytes=limit)


def _rms(x, g):
    ms = jnp.mean(x * x, axis=-1, keepdims=True)
    return x * lax.rsqrt(ms + EPS) * g


def _rope(x, tab_ref):
    w = x.shape[1]
    return (x * tab_ref[0]
            + pltpu.roll(x, w - 16, 1) * tab_ref[1]
            + pltpu.roll(x, 16, 1) * tab_ref[2])


def _softmax_rows(s):
    m = jnp.max(s, axis=-1, keepdims=True)
    e = jnp.exp(s - m)
    return e, jnp.sum(e, axis=-1, keepdims=True)


def _pre_even_kernel(x_ref, g_ref, win_ref, gq_ref, gkv_ref, wuq_ref, wukv_ref, tq_ref, tk_ref,
                     qa_ref, ka_ref, va_ref, qcat_ref, kcat_ref, vb_ref):
    hn = _rms(x_ref[...], g_ref[...]).astype(BF16)
    z = jnp.dot(hn, win_ref[...], preferred_element_type=F32)
    qa_ref[...] = (z[:, 0:512] * (HEAD_DIM ** -0.5)).astype(BF16)
    ka_ref[...] = z[:, 512:640].astype(BF16)
    va_ref[...] = z[:, 640:768].astype(BF16)
    cq = _rms(z[:, 768:1152], gq_ref[...]).astype(BF16)
    q = jnp.dot(cq, wuq_ref[...], preferred_element_type=F32)
    for m in range(4):
        qcat_ref[:, 256 * m:256 * (m + 1)] = _rope(q[:, 256 * m:256 * (m + 1)], tq_ref).astype(BF16)
    ckv = _rms(z[:, 1152:1408], gkv_ref[...]).astype(BF16)
    kv = jnp.dot(ckv, wukv_ref[...], preferred_element_type=F32)
    kr = _rope(z[:, 1408:1536], tk_ref).astype(BF16)
    for m in range(4):
        kcat_ref[:, 256 * m:256 * m + 128] = kv[:, 128 * m:128 * (m + 1)].astype(BF16)
        kcat_ref[:, 256 * m + 128:256 * (m + 1)] = kr
    vb_ref[...] = kv[:, 512:1024].astype(BF16)


def _pre_odd_kernel(x_ref, g_ref, win_ref, gqk_ref, ones_ref, tab_ref,
                    qc_ref, kc_ref, vc_ref, qd_ref, kd_ref, vd_ref):
    hn = _rms(x_ref[...], g_ref[...]).astype(BF16)
    z = jnp.dot(hn, win_ref[...], preferred_element_type=F32)
    qk = z[:, 0:640]
    ss = qk * qk
    hi = ss.astype(BF16)
    lo = (ss - hi.astype(F32)).astype(BF16)
    ms = (jnp.dot(hi, ones_ref[...], preferred_element_type=F32)
          + jnp.dot(lo, ones_ref[...], preferred_element_type=F32)) * (1.0 / HEAD_DIM)
    qkn = qk * lax.rsqrt(ms + EPS) * gqk_ref[...]
    for m in range(4):
        blk = _rope(qkn[:, 128 * m:128 * (m + 1)], tab_ref)
        qc_ref[:, 128 * m:128 * (m + 1)] = (blk * (LOG2E * HEAD_DIM ** -0.5)).astype(BF16)
    kc_ref[...] = _rope(qkn[:, 512:640], tab_ref).astype(BF16)
    vc_ref[...] = z[:, 640:768].astype(BF16)
    qd_ref[...] = (z[:, 768:1280] * (LOG2E * HEAD_DIM ** -0.5)).astype(BF16)
    kd_ref[...] = z[:, 1280:1792].astype(BF16)
    vd_ref[...] = z[:, 1792:2304].astype(BF16)


def _const_spec(shape):
    nd = len(shape)
    return pl.BlockSpec(shape, lambda *_: (0,) * nd)


def _pre_even(x, S, g, win, gq, gkv, wuq, wukv, tq, tk, tm):
    T = x.shape[0]
    ns = S // tm
    row = lambda w: pl.BlockSpec((tm, w), lambda i: (i, 0))
    tab = lambda w: pl.BlockSpec((3, tm, w), lambda i: (0, i % ns, 0))
    out_w = (512, 128, 128, 1024, 1024, 512)
    return pl.pallas_call(
        _pre_even_kernel,
        grid=(T // tm,),
        in_specs=[row(D_MODEL), _const_spec(g.shape), _const_spec(win.shape), _const_spec(gq.shape),
                  _const_spec(gkv.shape), _const_spec(wuq.shape), _const_spec(wukv.shape),
                  tab(256), tab(128)],
        out_specs=[row(w) for w in out_w],
        out_shape=[jax.ShapeDtypeStruct((T, w), BF16) for w in out_w],
        compiler_params=_cparams(),
        name="pre_even",
    )(x, g, win, gq, gkv, wuq, wukv, tq, tk)


def _pre_odd(x, S, g, win, gqk, ones, tab3, tm):
    T = x.shape[0]
    ns = S // tm
    row = lambda w: pl.BlockSpec((tm, w), lambda i: (i, 0))
    out_w = (512, 128, 128, 512, 512, 512)
    return pl.pallas_call(
        _pre_odd_kernel,
        grid=(T // tm,),
        in_specs=[row(D_MODEL), _const_spec(g.shape), _const_spec(win.shape), _const_spec(gqk.shape),
                  _const_spec(ones.shape), pl.BlockSpec((3, tm, 128), lambda i: (0, i % ns, 0))],
        out_specs=[row(w) for w in out_w],
        out_shape=[jax.ShapeDtypeStruct((T, w), BF16) for w in out_w],
        compiler_params=_cparams(),
        name="pre_odd",
    )(x, g, win, gqk, ones, tab3)


def _window_kernel(q_ref, kp_ref, kc_ref, kn_ref, vp_ref, vc_ref, vn_ref, bias_ref, sink_ref, o_ref):
    i = pl.program_id(1)
    nb = pl.num_programs(1)
    q = q_ref[...]
    tw = q.shape[0]
    kw = jnp.concatenate([kp_ref[...], kc_ref[...], kn_ref[...]], axis=0)
    vw = jnp.concatenate([vp_ref[...], vc_ref[...], vn_ref[...]], axis=0)
    col = lax.broadcasted_iota(jnp.int32, (1, tw + 2 * QB), 1)
    outside = ((col < QB) & (i == 0)) | ((col >= QB + tw) & (i == nb - 1))
    lane = lax.broadcasted_iota(jnp.int32, (1, LANES), 1)
    for m in range(4):
        qblk = q[:, 128 * m:128 * (m + 1)]
        outs = []
        for half in range(2):
            h = m + 4 * half
            keep = (lane < HEAD_DIM) if half == 0 else (lane >= HEAD_DIM)
            qm = jnp.where(keep, qblk, jnp.zeros_like(qblk))
            s = lax.dot_general(qm, kw, _NT, preferred_element_type=F32) + bias_ref[h]
            s = jnp.where(outside, NEG_INF, s)
            sk = sink_ref[h:h + 1, 0:1]
            mx = jnp.maximum(jnp.max(s, axis=-1, keepdims=True), sk)
            e = jnp.exp(s - mx)
            den = jnp.sum(e, axis=-1, keepdims=True) + jnp.exp(sk - mx)
            outs.append(jnp.dot(e.astype(BF16), vw, preferred_element_type=F32) / den)
        o_ref[:, 128 * m:128 * (m + 1)] = jnp.where(lane < HEAD_DIM, outs[0], outs[1]).astype(BF16)


def _window_attn(qa, ka, va, bias, sink, B, S, tw):
    T = qa.shape[0]
    nb = S // tw
    r = tw // QB
    nqb = S // QB
    side = lambda d: pl.BlockSpec(
        (QB, 128), lambda b, i: (b * nqb + jnp.clip(r * i + d, 0, nqb - 1), 0))
    mid = pl.BlockSpec((tw, 128), lambda b, i: (b * nb + i, 0))
    return pl.pallas_call(
        _window_kernel,
        grid=(B, nb),
        in_specs=[pl.BlockSpec((tw, 512), lambda b, i: (b * nb + i, 0)),
                  side(-1), mid, side(r), side(-1), mid, side(r),
                  _const_spec(bias.shape), _const_spec(sink.shape)],
        out_specs=pl.BlockSpec((tw, 512), lambda b, i: (b * nb + i, 0)),
        out_shape=jax.ShapeDtypeStruct((T, 512), BF16),
        compiler_params=_cparams(),
        name="window_attn",
    )(qa, ka, ka, ka, va, va, va, bias, sink)


def _pair_kernel(q_ref, k_ref, v_ref, o_ref, vx_ref, *, lanes_a, lanes_b, kc):
    @pl.when(pl.program_id(2) == 0)
    def _():
        vx_ref[:, 0:LANES] = v_ref[...]
        vx_ref[:, LANES:2 * LANES] = jnp.ones((vx_ref.shape[0], LANES), BF16)

    q = q_ref[...]
    tq, wq = q.shape
    S = k_ref.shape[0]
    lane = lax.broadcasted_iota(jnp.int32, (1, wq), 1)
    qms = []
    for ranges in (lanes_a, lanes_b):
        keep = functools.reduce(jnp.logical_or, [(lane >= lo) & (lane < hi) for lo, hi in ranges])
        qms.append(jnp.where(keep, q, jnp.zeros_like(q)))
    m = [jnp.full((tq, 1), NEG_INF, F32) for _ in range(2)]
    acc = [jnp.zeros((tq, 2 * LANES), F32) for _ in range(2)]
    for c in range(S // kc):
        kchunk = k_ref[kc * c:kc * (c + 1), :]
        vchunk = vx_ref[kc * c:kc * (c + 1), :]
        for h in range(2):
            s = lax.dot_general(qms[h], kchunk, _NT, preferred_element_type=F32)
            m_new = jnp.maximum(m[h], jnp.max(s, axis=-1, keepdims=True))
            p = jnp.exp2(s - m_new).astype(BF16)
            acc[h] = jnp.exp2(m[h] - m_new) * acc[h] + jnp.dot(p, vchunk, preferred_element_type=F32)
            m[h] = m_new
    outs = [a[:, 0:LANES] / a[:, LANES:LANES + 1] for a in acc]
    lane_o = lax.broadcasted_iota(jnp.int32, (1, LANES), 1)
    o_ref[...] = jnp.where(lane_o < 64, outs[0], outs[1]).astype(BF16)


def _pair_attn(q, k, v, B, S, tq, wq, shared_kv, lanes_a, lanes_b, name):
    T = q.shape[0]
    nq = S // tq
    kc = _tile(S, 512)
    kv_map = (lambda b, m, i: (b, 0)) if shared_kv else (lambda b, m, i: (b, m))
    return pl.pallas_call(
        functools.partial(_pair_kernel, lanes_a=lanes_a, lanes_b=lanes_b, kc=kc),
        scratch_shapes=[pltpu.VMEM((S, 2 * LANES), BF16)],
        grid=(B, 4, nq),
        in_specs=[pl.BlockSpec((tq, wq), lambda b, m, i: (b * nq + i, m)),
                  pl.BlockSpec((S, wq), kv_map),
                  pl.BlockSpec((S, 128), kv_map)],
        out_specs=pl.BlockSpec((tq, 128), lambda b, m, i: (b * nq + i, m)),
        out_shape=jax.ShapeDtypeStruct((T, 512), BF16),
        compiler_params=_cparams(),
        name=name,
    )(q, k, v)


def _diff_kernel(lam_ref, gsub_ref, q_ref, k_ref, v_ref, bias_ref, o_ref, vx_ref, *, S, tq, kc, lambda_init):
    i = pl.program_id(2)

    @pl.when(i == 0)
    def _():
        vx_ref[:, 0:LANES] = v_ref[...]
        vx_ref[:, LANES:2 * LANES] = jnp.ones((vx_ref.shape[0], LANES), BF16)

    r = tq // QB
    nb = kc // QB
    q = q_ref[...]
    lp = lam_ref[...]
    lam = (jnp.exp(jnp.sum(lp[0:1] * lp[1:2], axis=-1, keepdims=True))
           - jnp.exp(jnp.sum(lp[2:3] * lp[3:4], axis=-1, keepdims=True)) + lambda_init)
    lane = lax.broadcasted_iota(jnp.int32, (1, LANES), 1)
    qms = [jnp.where(lane < HEAD_DIM, q, jnp.zeros_like(q)), jnp.where(lane >= HEAD_DIM, q, jnp.zeros_like(q))]
    m = [jnp.full((tq, 1), NEG_INF, F32) for _ in range(2)]
    acc = [jnp.zeros((tq, 2 * LANES), F32) for _ in range(2)]
    for c in range(S // kc):
        kchunk = k_ref[kc * c:kc * (c + 1), :]
        vchunk = vx_ref[kc * c:kc * (c + 1), :]
        bias = jnp.concatenate(
            [bias_ref[0, jnp.clip(nb * c + j - r * i + 2, 0, r + 3)] for j in range(nb)], axis=1)
        for h in range(2):
            s = lax.dot_general(qms[h], kchunk, _NT, preferred_element_type=F32) + bias
            m_new = jnp.maximum(m[h], jnp.max(s, axis=-1, keepdims=True))
            p = jnp.exp2(s - m_new).astype(BF16)
            acc[h] = jnp.exp2(m[h] - m_new) * acc[h] + jnp.dot(p, vchunk, preferred_element_type=F32)
            m[h] = m_new
    outs = [a[:, 0:LANES] / a[:, LANES:LANES + 1] for a in acc]
    o = outs[0] - lam * outs[1]
    o_ref[...] = (_rms(o, gsub_ref[...]) * (1.0 - lambda_init)).astype(BF16)


def _diff_attn(qd, kd, vd, lam_p, gsub, bias, B, S, tq, lambda_init):
    T = qd.shape[0]
    nq = S // tq
    nt = bias.shape[1]
    kc = _tile(S, 512)
    return pl.pallas_call(
        functools.partial(_diff_kernel, S=S, tq=tq, kc=kc, lambda_init=lambda_init),
        grid=(B, D_HEADS, nq),
        in_specs=[_const_spec(lam_p.shape), _const_spec(gsub.shape),
                  pl.BlockSpec((tq, 128), lambda b, h, i: (b * nq + i, h)),
                  pl.BlockSpec((S, 128), lambda b, h, i: (b, h)),
                  pl.BlockSpec((S, 128), lambda b, h, i: (b, h)),
                  pl.BlockSpec((1, nt, tq, 128), lambda b, h, i: (h, 0, 0, 0))],
        out_specs=pl.BlockSpec((tq, 128), lambda b, h, i: (b * nq + i, h)),
        out_shape=jax.ShapeDtypeStruct((T, 512), BF16),
        scratch_shapes=[pltpu.VMEM((S, 2 * LANES), BF16)],
        compiler_params=_cparams(),
        name="diff_attn",
    )(lam_p, gsub, qd, kd, vd, bias)


def _norm_matmul_kernel(x_ref, g_ref, w_ref, o_ref):
    hn = _rms(x_ref[...], g_ref[...]).astype(BF16)
    o_ref[...] = jnp.dot(hn, w_ref[...], preferred_element_type=F32).astype(o_ref.dtype)


def _norm_matmul(x, g, w, tm, out_dtype):
    T, K = x.shape
    N = w.shape[1]
    return pl.pallas_call(
        _norm_matmul_kernel,
        grid=(T // tm,),
        in_specs=[pl.BlockSpec((tm, K), lambda i: (i, 0)), _const_spec(g.shape), _const_spec(w.shape)],
        out_specs=pl.BlockSpec((tm, N), lambda i: (i, 0)),
        out_shape=jax.ShapeDtypeStruct((T, N), out_dtype),
        compiler_params=_cparams(),
        name="norm_matmul",
    )(x, g, w)


def _post_kernel(x_ref, oa_ref, ob_ref, wout_ref, gx_ref, wq_ref, kv_ref, wo_ref, gf_ref, wr_ref,
                 x2_ref, hn_ref, aff_ref):
    x1 = (x_ref[...]
          + jnp.dot(oa_ref[...], wout_ref[0:512, :], preferred_element_type=F32)
          + jnp.dot(ob_ref[...], wout_ref[512:1024, :], preferred_element_type=F32))
    hc = _rms(x1, gx_ref[...]).astype(BF16)
    q = (jnp.dot(hc, wq_ref[...], preferred_element_type=F32) * (X_HEAD_DIM ** -0.5)).astype(BF16)
    heads = []
    for h in range(X_HEADS):
        lo = X_HEAD_DIM * h
        kh = kv_ref[:, lo:lo + X_HEAD_DIM]
        vh = kv_ref[:, D_MODEL + lo:D_MODEL + lo + X_HEAD_DIM]
        s = lax.dot_general(q[:, lo:lo + X_HEAD_DIM], kh, _NT, preferred_element_type=F32)
        e, l = _softmax_rows(s)
        heads.append((jnp.dot(e.astype(BF16), vh, preferred_element_type=F32) / l).astype(BF16))
    o = jnp.concatenate(heads, axis=1)
    x2 = x1 + jnp.dot(o, wo_ref[...], preferred_element_type=F32)
    x2_ref[...] = x2
    hf = _rms(x2, gf_ref[...])
    hn_ref[...] = hf.astype(BF16)
    tm = hf.shape[0]
    h_hi = hf.astype(BF16)
    h_lo = (hf - h_hi.astype(F32)).astype(BF16)
    r = jnp.dot(jnp.concatenate([h_hi, h_lo], axis=0), wr_ref[...], preferred_element_type=F32)
    top = r[0:tm]
    logits = top + pltpu.roll(top, LANES - N_EXPERTS, 1) + r[tm:2 * tm]
    valid = lax.broadcasted_iota(jnp.int32, (1, LANES), 1) < N_EXPERTS
    mx = jnp.max(jnp.where(valid, logits, NEG_INF), axis=-1, keepdims=True)
    e = jnp.where(valid, jnp.exp(logits - mx), 0.0)
    aff = e / jnp.sum(e, axis=-1, keepdims=True)
    aff_t = aff.T
    for c in range(tm // LANES):
        aff_ref[c] = aff_t[0:N_EXPERTS, LANES * c:LANES * (c + 1)]


def _post(x, oa, ob, wout, gx, wq, kv, wo, gf, wr3, S, n_mem, tm):
    T = x.shape[0]
    per_b = S // tm
    row = lambda w: pl.BlockSpec((tm, w), lambda i: (i, 0))
    return pl.pallas_call(
        _post_kernel,
        grid=(T // tm,),
        in_specs=[row(D_MODEL), row(512), row(512), _const_spec(wout.shape), _const_spec(gx.shape),
                  _const_spec(wq.shape), pl.BlockSpec((n_mem, 2 * D_MODEL), lambda i: (i // per_b, 0)),
                  _const_spec(wo.shape), _const_spec(gf.shape), _const_spec(wr3.shape)],
        out_specs=[row(D_MODEL), row(D_MODEL),
                   pl.BlockSpec((tm // LANES, N_EXPERTS, LANES), lambda i: (i, 0, 0))],
        out_shape=[jax.ShapeDtypeStruct((T, D_MODEL), F32), jax.ShapeDtypeStruct((T, D_MODEL), BF16),
                   jax.ShapeDtypeStruct((T // LANES, N_EXPERTS, LANES), F32)],
        compiler_params=_cparams(),
        name="post",
    )(x, oa, ob, wout, gx, wq, kv, wo, gf, wr3)


def _route_kernel(aff_ref, tri_ref, ones_ref, carrymat_ref, key_ref, carry_ref, *, cap):
    x = aff_ref[...]
    nj = x.shape[0]

    def count(mask):
        c = jnp.sum(mask.astype(F32), axis=0, keepdims=True)
        return jnp.sum(c, axis=2, keepdims=True)

    def bisect(b, thr):
        cand = thr | lax.shift_left(jnp.int32(1), 30 - b)
        return jnp.where(count(x >= pltpu.bitcast(cand, F32)) >= cap, cand, thr)

    thr = pltpu.bitcast(lax.fori_loop(0, 31, bisect, jnp.zeros((1, N_EXPERTS, 1), jnp.int32)), F32)
    gt = x > thr
    eq = x == thr
    need = cap - count(gt)

    def prefix(mask):
        v = mask.astype(BF16).reshape(nj * N_EXPERTS, LANES)
        within = jnp.dot(v, tri_ref[...], preferred_element_type=F32)
        tot = jnp.dot(v, ones_ref[...], preferred_element_type=F32)
        carry = jnp.dot(carrymat_ref[...], tot.astype(BF16), preferred_element_type=F32)
        return within + carry, carry

    eq_rank, _ = prefix(eq)
    sel = gt | (eq & (eq_rank.reshape(nj, N_EXPERTS, LANES) < need))
    rank, carry = prefix(sel)
    key_ref[...] = jnp.where(sel, rank.reshape(nj, N_EXPERTS, LANES), -1.0)
    carry_ref[...] = carry


def _route(aff3, tri, ones, carrymat, cap):
    nj = aff3.shape[0]
    rows = nj * N_EXPERTS
    return pl.pallas_call(
        functools.partial(_route_kernel, cap=cap),
        out_shape=[jax.ShapeDtypeStruct((nj, N_EXPERTS, LANES), F32),
                   jax.ShapeDtypeStruct((rows, LANES), F32)],
        compiler_params=_cparams(),
        name="route",
    )(aff3, tri, ones, carrymat)


def _window_start(c0, cap, win):
    a = jnp.minimum((c0 // 16) * 16, cap - win)
    return pl.multiple_of(a, 16)


GATHER_EXPERTS = 2


def _gather_kernel(c0_ref, key_ref, aff_ref, hn_ref, xe_ref, gs_ref, *, cap, win, nq):
    sb = pl.program_id(1)

    @pl.when(sb == 0)
    def _():
        xe_ref[...] = jnp.zeros_like(xe_ref)
        gs_ref[...] = jnp.zeros_like(gs_ref)

    slot = lax.broadcasted_iota(jnp.int32, (win, 1), 0).astype(F32)

    def lane_row(ref, j, q):
        return jnp.concatenate([ref[j, 2 * q:2 * q + 1, :], ref[j, 2 * q + 1:2 * q + 2, :]], axis=1)

    def add_window(j, q, a, lo):
        keyrow = lane_row(key_ref, j, q)
        hit = (keyrow - a.astype(F32) == slot) & (keyrow >= lo.astype(F32))
        rows = jnp.dot(hit.astype(BF16), hn_ref[256 * q:256 * (q + 1), :],
                       preferred_element_type=F32)
        xe_ref[j, pl.ds(a, win), :] = xe_ref[j, pl.ds(a, win), :] + rows.astype(BF16)
        gates = jnp.sum(jnp.where(hit, lane_row(aff_ref, j, q), 0.0), axis=-1, keepdims=True)
        gs_ref[j, pl.ds(a, win), :] = gs_ref[j, pl.ds(a, win), :] + gates

    experts = [pl.program_id(0) * GATHER_EXPERTS + j for j in range(GATHER_EXPERTS)]
    starts = [[_window_start(c0_ref[e, sb * nq + q], cap, win) for q in range(nq)] for e in experts]
    for q in range(nq):
        for j, e in enumerate(experts):
            add_window(j, q, starts[j][q], c0_ref[e, sb * nq + q])
    extra_windows = -(-(256 + 15) // win) - 1
    for q in range(nq):
        for j, e in enumerate(experts):
            for w in range(1, extra_windows + 1):
                nxt = starts[j][q] + w * win
                pl.when(c0_ref[e, sb * nq + q + 1] > nxt)(
                    functools.partial(add_window, j, q, _window_start(nxt, cap, win), nxt))


def _gather(c0, key_em, aff_em, hn, cap, win, tsb):
    T, D = hn.shape
    nq = tsb // 256
    ge = GATHER_EXPERTS
    idx_spec = pl.BlockSpec((ge, tsb // LANES, LANES), lambda e, sb, c0: (e, sb, 0))
    grid_spec = pltpu.PrefetchScalarGridSpec(
        num_scalar_prefetch=1,
        grid=(N_EXPERTS // ge, T // tsb),
        in_specs=[idx_spec, idx_spec, pl.BlockSpec((tsb, D), lambda e, sb, c0: (sb, 0))],
        out_specs=[pl.BlockSpec((ge, cap, D), lambda e, sb, c0: (e, 0, 0)),
                   pl.BlockSpec((ge, cap, LANES), lambda e, sb, c0: (e, 0, 0))],
    )
    return pl.pallas_call(
        functools.partial(_gather_kernel, cap=cap, win=win, nq=nq),
        grid_spec=grid_spec,
        out_shape=[jax.ShapeDtypeStruct((N_EXPERTS, cap, D), BF16),
                   jax.ShapeDtypeStruct((N_EXPERTS, cap, LANES), F32)],
        compiler_params=_cparams(),
        name="moe_gather",
    )(c0, key_em, aff_em, hn)


def _ffn_kernel(xe_ref, gs_ref, wg_ref, wu_ref, wd_ref, y_ref, acc_ref):
    f = pl.program_id(2)

    @pl.when(f == 0)
    def _():
        acc_ref[...] = jnp.zeros_like(acc_ref)

    xe = xe_ref[0]
    a = jnp.dot(xe, wg_ref[0, 0].astype(BF16), preferred_element_type=F32)
    u = jnp.dot(xe, wu_ref[0, 0].astype(BF16), preferred_element_type=F32)
    h = (a * (1.0 / (1.0 + jnp.exp(-a))) * u).astype(BF16)
    acc_ref[...] += jnp.dot(h, wd_ref[0, 0].astype(BF16), preferred_element_type=F32)

    @pl.when(f == pl.num_programs(2) - 1)
    def _():
        y_ref[0] = (acc_ref[...] * gs_ref[0][:, 0:1]).astype(BF16)


def _expert_ffn(xe, gs, wg, wu, wd, layer, tc, tf):
    E, cap, D = xe.shape
    F = wg.shape[3]
    return pl.pallas_call(
        _ffn_kernel,
        grid=(E, cap // tc, F // tf),
        in_specs=[pl.BlockSpec((1, tc, D), lambda e, c, f: (e, c, 0)),
                  pl.BlockSpec((1, tc, LANES), lambda e, c, f: (e, c, 0)),
                  pl.BlockSpec((1, 1, D, tf), lambda e, c, f: (layer, e, 0, f)),
                  pl.BlockSpec((1, 1, D, tf), lambda e, c, f: (layer, e, 0, f)),
                  pl.BlockSpec((1, 1, tf, D), lambda e, c, f: (layer, e, f, 0))],
        out_specs=pl.BlockSpec((1, tc, D), lambda e, c, f: (e, c, 0)),
        out_shape=jax.ShapeDtypeStruct((E, cap, D), BF16),
        scratch_shapes=[pltpu.VMEM((tc, D), F32)],
        compiler_params=_cparams(),
        name="expert_ffn",
    )(xe, gs, wg, wu, wd)


COMBINE_EXPERTS = 2


def _combine_kernel(c0_ref, x_ref, key_ref, y_ref, o_ref, *, cap, win, nq):
    sb = pl.program_id(0)

    @pl.when(pl.program_id(1) == 0)
    def _():
        o_ref[...] = x_ref[...]

    experts = [pl.program_id(1) * COMBINE_EXPERTS + j for j in range(COMBINE_EXPERTS)]
    lane = lax.broadcasted_iota(jnp.int32, (1, N_EXPERTS), 1)
    keycols = [jnp.sum(jnp.where(lane == e, key_ref[...], 0.0), axis=-1, keepdims=True) for e in experts]
    slot = lax.broadcasted_iota(jnp.int32, (1, win), 1).astype(F32)

    def one_hot(j, q, a, lo):
        kq = keycols[j][256 * q:256 * (q + 1)]
        hit = kq - a.astype(F32) == slot
        if lo is not None:
            hit = hit & (kq >= lo.astype(F32))
        return hit.astype(BF16)

    starts = [[_window_start(c0_ref[e, sb * nq + q], cap, win) for q in range(nq)] for e in experts]
    for q in range(nq):
        hits = jnp.concatenate([one_hot(j, q, starts[j][q], None) for j in range(COMBINE_EXPERTS)], axis=1)
        rows = jnp.concatenate([y_ref[j, pl.ds(starts[j][q], win), :] for j in range(COMBINE_EXPERTS)], axis=0)
        o_ref[256 * q:256 * (q + 1), :] = (o_ref[256 * q:256 * (q + 1), :]
                                           + jnp.dot(hits, rows, preferred_element_type=F32))
    extra_windows = -(-(256 + 15) // win) - 1
    for q in range(nq):
        for j, e in enumerate(experts):
            for w in range(1, extra_windows + 1):
                nxt = starts[j][q] + w * win

                @pl.when(c0_ref[e, sb * nq + q + 1] > nxt)
                def _(j=j, q=q, nxt=nxt):
                    a = _window_start(nxt, cap, win)
                    o_ref[256 * q:256 * (q + 1), :] = (
                        o_ref[256 * q:256 * (q + 1), :]
                        + jnp.dot(one_hot(j, q, a, nxt), y_ref[j, pl.ds(a, win), :], preferred_element_type=F32))


def _combine(c0, x, key_tm, y, cap, win, tsb):
    T, D = x.shape
    nq = tsb // 256
    ce = COMBINE_EXPERTS
    grid_spec = pltpu.PrefetchScalarGridSpec(
        num_scalar_prefetch=1,
        grid=(T // tsb, N_EXPERTS // ce),
        in_specs=[pl.BlockSpec((tsb, D), lambda sb, e, c0: (sb, 0), pipeline_mode=pl.Buffered(1)),
                  pl.BlockSpec((tsb, N_EXPERTS), lambda sb, e, c0: (sb, 0)),
                  pl.BlockSpec((ce, cap, D), lambda sb, e, c0: (e, 0, 0))],
        out_specs=pl.BlockSpec((tsb, D), lambda sb, e, c0: (sb, 0)),
    )
    return pl.pallas_call(
        functools.partial(_combine_kernel, cap=cap, win=win, nq=nq),
        grid_spec=grid_spec,
        out_shape=jax.ShapeDtypeStruct((T, D), F32),
        compiler_params=_cparams(VMEM_LIMIT_BIG),
        name="moe_combine",
    )(c0, x, key_tm, y)


def _route_constants(nj):
    lane = jnp.arange(LANES)
    tri = (lane[:, None] < lane[None, :]).astype(BF16)
    ones = jnp.ones((LANES, LANES), BF16)
    r = jnp.arange(nj * N_EXPERTS)
    carrymat = ((r[None, :] % N_EXPERTS == r[:, None] % N_EXPERTS) & (r[None, :] < r[:, None])).astype(BF16)
    return tri, ones, carrymat


def _moe(x2, hn, aff3, wg, wu, wd, layer, cap, tc, tf):
    T, D = x2.shape
    nj = T // LANES
    win = min(128, cap)
    tsb = min(2048, T)
    key3, carry = _route(aff3, *_route_constants(nj), cap)
    key_em = key3.transpose(1, 0, 2)
    aff_em = aff3.transpose(1, 0, 2)
    key_tm = key3.transpose(0, 2, 1).reshape(T, N_EXPERTS)
    starts = carry[:, 0].reshape(nj, N_EXPERTS)[::2].T.astype(jnp.int32)
    c0 = jnp.concatenate([starts, jnp.full((N_EXPERTS, 1), cap, jnp.int32)], axis=1)
    xe, gs = _gather(c0, key_em, aff_em, hn, cap, win, tsb)
    y = _expert_ffn(xe, gs, wg, wu, wd, layer, tc, tf)
    return _combine(c0, x2, key_tm, y, cap, win, tsb)


def _final_norm_kernel(x_ref, g_ref, o_ref):
    o_ref[...] = _rms(x_ref[...], g_ref[...])


def _final_norm(x, g, tm):
    T, D = x.shape
    return pl.pallas_call(
        _final_norm_kernel,
        grid=(T // tm,),
        in_specs=[pl.BlockSpec((tm, D), lambda i: (i, 0)), _const_spec(g.shape)],
        out_specs=pl.BlockSpec((tm, D), lambda i: (i, 0)),
        out_shape=jax.ShapeDtypeStruct((T, D), F32),
        name="final_norm",
    )(x, g)


def _t5_bucket(rel):
    half = N_BUCKETS // 2
    exact = half // 2
    n = jnp.abs(rel)
    big = exact + (jnp.log(jnp.maximum(n, 1).astype(F32) / exact)
                   / math.log(MAX_DISTANCE / exact) * (half - exact)).astype(jnp.int32)
    big = jnp.minimum(big, half - 1)
    return jnp.where(rel > 0, half, 0) + jnp.where(n < exact, n, big)


def _rope_freqs(pos, dim):
    freqs = ROPE_BASE ** (-jnp.arange(0, dim, 2, dtype=F32) / dim)
    ang = pos.astype(F32)[:, None] * freqs[None, :]
    return jnp.cos(ang), jnp.sin(ang)


def _rope_lane_table(cos, sin, active):
    w = cos.shape[1]
    first = (jnp.arange(w) % 32) < 16
    c = jnp.where(active[None, :], cos, 1.0)
    s1 = jnp.where((active & first)[None, :], -sin, 0.0)
    s2 = jnp.where((active & ~first)[None, :], sin, 0.0)
    return jnp.stack([c, s1, s2]).astype(F32)


def _pair_heads_cols(w):
    k = w.shape[0]
    return w.reshape(k, 2, 4, HEAD_DIM).transpose(0, 2, 1, 3).reshape(k, 8 * HEAD_DIM)


def _pair_heads_rows(w):
    n = w.shape[1]
    return w.reshape(2, 4, HEAD_DIM, n).transpose(1, 0, 2, 3).reshape(8 * HEAD_DIM, n)


def _bias_lookup(table, bucket):
    shape = (table.shape[1],) + (1,) * bucket.ndim
    out = jnp.zeros((table.shape[1],) + bucket.shape, F32)
    for b in range(N_BUCKETS):
        out = jnp.where(bucket[None] == b, table[b].reshape(shape), out)
    return out


def _tables(S, tq_diff, tw, rel_bias):
    t = jnp.arange(S)
    lane128 = jnp.arange(128)
    lane256 = jnp.arange(256)
    cos_t, sin_t = _rope_freqs(t, B_ROPE)
    scale_b = LOG2E * (B_NOPE + B_ROPE) ** -0.5
    tq = _rope_lane_table(jnp.tile(cos_t, (1, 16)), jnp.tile(sin_t, (1, 16)),
                          (lane256 >= 128) & (lane256 < 192)) * scale_b
    tk = _rope_lane_table(jnp.tile(cos_t, (1, 8)), jnp.tile(sin_t, (1, 8)), lane128 < 64)
    cos_r, sin_r = _rope_freqs(t // GRID_W, HEAD_DIM // 2)
    cos_c, sin_c = _rope_freqs(t % GRID_W, HEAD_DIM // 2)
    cos_ax = jnp.tile(jnp.concatenate([cos_r, cos_r, cos_c, cos_c], axis=1), (1, 2))
    sin_ax = jnp.tile(jnp.concatenate([sin_r, sin_r, sin_c, sin_c], axis=1), (1, 2))
    tax = _rope_lane_table(cos_ax, sin_ax, jnp.ones((128,), bool))
    table = rel_bias.astype(F32)
    rel_win = jnp.arange(tw + 2 * QB)[None, :] - QB - jnp.arange(tw)[:, None]
    a_bias = _bias_lookup(table[:, :A_HEADS], _t5_bucket(rel_win))
    a_bias = jnp.where((jnp.abs(rel_win) <= WINDOW)[None], a_bias, NEG_INF)
    r = tq_diff // QB
    rel_d = (QB * (jnp.arange(r + 4)[:, None, None] - 2)
             + jnp.arange(QB)[None, None, :] - jnp.arange(tq_diff)[None, :, None])
    d_bias = _bias_lookup(table[:, A_HEADS:], _t5_bucket(rel_d)) * LOG2E
    return dict(tq=tq, tk=tk, tax=tax, a_bias=a_bias, d_bias=d_bias)


def _prep_weights(p):
    row = lambda v: v.reshape(1, -1).astype(F32)
    L = {}
    L['norm_mix'] = [row(p['norm_mix'][l]) for l in range(DEPTH)]
    L['norm_cross'] = [row(p['norm_cross'][l]) for l in range(DEPTH)]
    L['norm_mem'] = [row(p['norm_mem'][l]) for l in range(DEPTH)]
    L['norm_ffn'] = [row(p['norm_ffn'][l]) for l in range(DEPTH)]
    even, odd = [], []
    for i in range((DEPTH + 1) // 2):
        w = p['w_in_even'][i]
        kr = w[:, 1408:1440]
        win = jnp.concatenate([_pair_heads_cols(w[:, 0:512]), w[:, 512:1408], kr, kr,
                               jnp.zeros((D_MODEL, 64), w.dtype)], axis=1).astype(BF16)
        uq = p['b_w_uq'][i].reshape(B_Q_LORA, B_HEADS, B_NOPE + B_ROPE)
        blocks = []
        for m in range(4):
            blocks += [uq[:, 2 * m, :B_NOPE], uq[:, 2 * m + 1, :B_NOPE],
                       uq[:, 2 * m, B_NOPE:], uq[:, 2 * m + 1, B_NOPE:],
                       jnp.zeros((B_Q_LORA, 64), uq.dtype)]
        wuq = jnp.concatenate(blocks, axis=1).astype(BF16)
        ukv = p['b_w_ukv'][i].reshape(B_KV_LORA, B_HEADS, B_NOPE + B_V)
        wukv = jnp.concatenate([ukv[:, :, :B_NOPE].reshape(B_KV_LORA, -1),
                                ukv[:, :, B_NOPE:].reshape(B_KV_LORA, -1)], axis=1).astype(BF16)
        wo = p['w_out_even'][i]
        wout = jnp.concatenate([_pair_heads_rows(wo[0:512]), wo[512:1024]], axis=0).astype(BF16)
        sink = jnp.broadcast_to(p['a_sink'][i].astype(F32)[:, None], (A_HEADS, LANES))
        even.append(dict(win=win, wuq=wuq, wukv=wukv, wout=wout, sink=sink,
                         gq=row(p['b_q_norm'][i]), gkv=row(p['b_kv_norm'][i])))
    for i in range(DEPTH // 2):
        w = p['w_in_odd'][i]
        win = jnp.concatenate([_pair_heads_cols(w[:, 0:512]), w[:, 512:]], axis=1).astype(BF16)
        gqk = jnp.concatenate([jnp.tile(p['c_q_norm'][i], C_HEADS),
                               jnp.tile(p['c_k_norm'][i], C_KV_HEADS)]).reshape(1, -1).astype(F32)
        wo = p['w_out_odd'][i]
        wout = jnp.concatenate([_pair_heads_rows(wo[0:512]), wo[512:1024]], axis=0).astype(BF16)
        odd.append(dict(win=win, gqk=gqk, wout=wout, lam=p['d_lambda'][i].astype(F32),
                        gsub=row(p['d_subln'][i])))
    L['even'], L['odd'] = even, odd
    seg = jnp.arange(640) // HEAD_DIM
    L['ones'] = (seg[:, None] == seg[None, :]).astype(BF16)
    L['x_wq'] = [p['x_wq'][l].astype(BF16) for l in range(DEPTH)]
    L['x_wkv'] = [p['x_wkv'][l].astype(BF16) for l in range(DEPTH)]
    L['x_wo'] = [p['x_wo'][l].astype(BF16) for l in range(DEPTH)]
    wr = []
    for l in range(DEPTH):
        w = p['router'][l].astype(F32)
        w_hi = w.astype(BF16)
        w_lo = (w - w_hi.astype(F32)).astype(BF16)
        wr.append(jnp.concatenate([w_hi, w_lo, jnp.zeros((D_MODEL, LANES - 2 * N_EXPERTS), BF16)], axis=1))
    L['router'] = wr
    L['norm_final'] = row(p['norm_final'])
    return L


def _tile(n, pref):
    t = min(n, pref)
    assert n % t == 0, (n, t)
    return t


def _trunk(x, mem, p, L):
    B, S, D = x.shape
    n_mem = mem.shape[1]
    T = B * S
    tm_pre = _tile(S, 1024)
    tm_post = _tile(S, 512)
    tq = _tile(S, 512)
    tq_pair = _tile(S, 1024)
    cap = EC_CAPACITY * T // N_EXPERTS
    tc = _tile(cap, 1024)
    tf = 512
    tw = _tile(S, 256)
    tabs = _tables(S, tq, tw, p['rel_bias'])
    xf = x.reshape(T, D).astype(F32)
    memf = mem.reshape(B * n_mem, D).astype(F32)
    tm_mem = _tile(B * n_mem, 256)
    mla_a = ((0, 64), (128, 160))
    mla_b = ((64, 128), (160, 192))
    for layer in range(DEPTH):
        i = layer // 2
        if layer % 2 == 0:
            w = L['even'][i]
            qa, ka, va, qcat, kcat, vb = _pre_even(xf, S, L['norm_mix'][layer], w['win'], w['gq'], w['gkv'],
                                                   w['wuq'], w['wukv'], tabs['tq'], tabs['tk'], tm_pre)
            o1 = _window_attn(qa, ka, va, tabs['a_bias'], w['sink'], B, S, tw)
            o2 = _pair_attn(qcat, kcat, vb, B, S, tq_pair, 256, False, mla_a, mla_b, "latent_attn")
        else:
            w = L['odd'][i]
            qc, kc, vc, qd, kd, vd = _pre_odd(xf, S, L['norm_mix'][layer], w['win'], w['gqk'], L['ones'],
                                              tabs['tax'], tm_pre)
            o1 = _pair_attn(qc, kc, vc, B, S, tq_pair, 128, True, ((0, 64),), ((64, 128),), "axial_attn")
            lambda_init = 0.8 - 0.6 * math.exp(-0.3 * layer)
            o2 = _diff_attn(qd, kd, vd, w['lam'], w['gsub'], tabs['d_bias'], B, S, tq, lambda_init)
        kv = _norm_matmul(memf, L['norm_mem'][layer], L['x_wkv'][layer], tm_mem, BF16)
        x2, hn, aff = _post(xf, o1, o2, w['wout'], L['norm_cross'][layer], L['x_wq'][layer], kv,
                            L['x_wo'][layer], L['norm_ffn'][layer], L['router'][layer], S, n_mem, tm_post)
        xf = _moe(x2, hn, aff, p['e_w_gate'], p['e_w_up'], p['e_w_down'], layer, cap, tc, tf)
    out = _final_norm(xf, L['norm_final'], tm_pre)
    return out.reshape(B, S, D)


def kernel(x_prompt, x_sample, mem_prompt, mem_sample, rel_bias, norm_mix, norm_cross, norm_mem, norm_ffn,
           norm_final, w_in_even, a_sink, b_q_norm, b_kv_norm, b_w_uq, b_w_ukv, w_out_even, w_in_odd,
           c_q_norm, c_k_norm, d_lambda, d_subln, w_out_odd, x_wq, x_wkv, x_wo, router, e_w_gate,
           e_w_up, e_w_down):
    p = dict(rel_bias=rel_bias, norm_mix=norm_mix, norm_cross=norm_cross, norm_mem=norm_mem,
             norm_ffn=norm_ffn, norm_final=norm_final, w_in_even=w_in_even, a_sink=a_sink,
             b_q_norm=b_q_norm, b_kv_norm=b_kv_norm, b_w_uq=b_w_uq, b_w_ukv=b_w_ukv,
             w_out_even=w_out_even, w_in_odd=w_in_odd, c_q_norm=c_q_norm, c_k_norm=c_k_norm,
             d_lambda=d_lambda, d_subln=d_subln, w_out_odd=w_out_odd, x_wq=x_wq, x_wkv=x_wkv,
             x_wo=x_wo, router=router, e_w_gate=e_w_gate, e_w_up=e_w_up, e_w_down=e_w_down)
    L = _prep_weights(p)
    y_prompt = _trunk(x_prompt, mem_prompt, p, L)
    y_sample = _trunk(x_sample, mem_sample, p, L)
    return (y_prompt, y_sample)
```

```python
import functools
import math

import jax
import jax.numpy as jnp
from jax import lax
from jax.experimental import pallas as pl
from jax.experimental.pallas import tpu as pltpu

F32 = jnp.float32
BF16 = jnp.bfloat16

D_MODEL = 1024
DEPTH = 4
HEAD_DIM = 64
QB = 128
WINDOW = 128
A_HEADS = 8
A_KV_HEADS = 2
B_HEADS = 8
B_Q_LORA = 384
B_KV_LORA = 256
B_NOPE = 64
B_ROPE = 32
B_V = 64
C_HEADS = 8
C_KV_HEADS = 2
D_HEADS = 4
X_HEADS = 4
X_HEAD_DIM = D_MODEL // X_HEADS
N_BUCKETS = 32
MAX_DISTANCE = 128
N_EXPERTS = 16
EXPERT_FF = 2 * D_MODEL
EC_CAPACITY = 2
GRID_W = 64
ROPE_BASE = 10000.0
NEG_INF = -1e30
EPS = 1e-6
LOG2E = math.log2(math.e)

LANES = 128
VMEM_LIMIT = 48 * 1024 * 1024

_NT = (((1,), (1,)), ((), ()))


def _cparams(limit=VMEM_LIMIT):
    return pltpu.CompilerParams(vmem_limit_bytes=limit)


def _rms(x, g):
    ms = jnp.mean(x * x, axis=-1, keepdims=True)
    return x * lax.rsqrt(ms + EPS) * g


def _rope(x, tab_ref):
    w = x.shape[1]
    return (x * tab_ref[0]
            + pltpu.roll(x, w - 16, 1) * tab_ref[1]
            + pltpu.roll(x, 16, 1) * tab_ref[2])


def _softmax_rows(s):
    m = jnp.max(s, axis=-1, keepdims=True)
    e = jnp.exp(s - m)
    return e, jnp.sum(e, axis=-1, keepdims=True)


def _pre_even_kernel(x_ref, g_ref, win_ref, gq_ref, gkv_ref, wuq_ref, wukv_ref, tq_ref, tk_ref,
                     qa_ref, ka_ref, va_ref, qcat_ref, kcat_ref, vb_ref):
    hn = _rms(x_ref[...], g_ref[...]).astype(BF16)
    z = jnp.dot(hn, win_ref[...], preferred_element_type=F32)
    qa_ref[...] = (z[:, 0:512] * (HEAD_DIM ** -0.5)).astype(BF16)
    ka_ref[...] = z[:, 512:640].astype(BF16)
    va_ref[...] = z[:, 640:768].astype(BF16)
    cq = _rms(z[:, 768:1152], gq_ref[...]).astype(BF16)
    q = jnp.dot(cq, wuq_ref[...], preferred_element_type=F32)
    for m in range(4):
        qcat_ref[:, 256 * m:256 * (m + 1)] = _rope(q[:, 256 * m:256 * (m + 1)], tq_ref).astype(BF16)
    ckv = _rms(z[:, 1152:1408], gkv_ref[...]).astype(BF16)
    kv = jnp.dot(ckv, wukv_ref[...], preferred_element_type=F32)
    kr = _rope(z[:, 1408:1536], tk_ref).astype(BF16)
    for m in range(4):
        kcat_ref[:, 256 * m:256 * m + 128] = kv[:, 128 * m:128 * (m + 1)].astype(BF16)
        kcat_ref[:, 256 * m + 128:256 * (m + 1)] = kr
    vb_ref[...] = kv[:, 512:1024].astype(BF16)


def _pre_odd_kernel(x_ref, g_ref, win_ref, gqk_ref, ones_ref, tab_ref,
                    qc_ref, kc_ref, vc_ref, qd_ref, kd_ref, vd_ref):
    hn = _rms(x_ref[...], g_ref[...]).astype(BF16)
    z = jnp.dot(hn, win_ref[...], preferred_element_type=F32)
    qk = z[:, 0:640]
    ss = qk * qk
    hi = ss.astype(BF16)
    lo = (ss - hi.astype(F32)).astype(BF16)
    ms = (jnp.dot(hi, ones_ref[...], preferred_element_type=F32)
          + jnp.dot(lo, ones_ref[...], preferred_element_type=F32)) * (1.0 / HEAD_DIM)
    qkn = qk * lax.rsqrt(ms + EPS) * gqk_ref[...]
    for m in range(4):
        blk = _rope(qkn[:, 128 * m:128 * (m + 1)], tab_ref)
        qc_ref[:, 128 * m:128 * (m + 1)] = (blk * (LOG2E * HEAD_DIM ** -0.5)).astype(BF16)
    kc_ref[...] = _rope(qkn[:, 512:640], tab_ref).astype(BF16)
    vc_ref[...] = z[:, 640:768].astype(BF16)
    qd_ref[...] = (z[:, 768:1280] * (LOG2E * HEAD_DIM ** -0.5)).astype(BF16)
    kd_ref[...] = z[:, 1280:1792].astype(BF16)
    vd_ref[...] = z[:, 1792:2304].astype(BF16)


def _const_spec(shape):
    nd = len(shape)
    return pl.BlockSpec(shape, lambda *_: (0,) * nd)


def _pre_even(x, S, g, win, gq, gkv, wuq, wukv, tq, tk, tm):
    T = x.shape[0]
    ns = S // tm
    row = lambda w: pl.BlockSpec((tm, w), lambda i: (i, 0))
    tab = lambda w: pl.BlockSpec((3, tm, w), lambda i: (0, i % ns, 0))
    out_w = (512, 128, 128, 1024, 1024, 512)
    return pl.pallas_call(
        _pre_even_kernel,
        grid=(T // tm,),
        in_specs=[row(D_MODEL), _const_spec(g.shape), _const_spec(win.shape), _const_spec(gq.shape),
                  _const_spec(gkv.shape), _const_spec(wuq.shape), _const_spec(wukv.shape),
                  tab(256), tab(128)],
        out_specs=[row(w) for w in out_w],
        out_shape=[jax.ShapeDtypeStruct((T, w), BF16) for w in out_w],
        compiler_params=_cparams(),
        name="pre_even",
    )(x, g, win, gq, gkv, wuq, wukv, tq, tk)


def _pre_odd(x, S, g, win, gqk, ones, tab3, tm):
    T = x.shape[0]
    ns = S // tm
    row = lambda w: pl.BlockSpec((tm, w), lambda i: (i, 0))
    out_w = (512, 128, 128, 512, 512, 512)
    return pl.pallas_call(
        _pre_odd_kernel,
        grid=(T // tm,),
        in_specs=[row(D_MODEL), _const_spec(g.shape), _const_spec(win.shape), _const_spec(gqk.shape),
                  _const_spec(ones.shape), pl.BlockSpec((3, tm, 128), lambda i: (0, i % ns, 0))],
        out_specs=[row(w) for w in out_w],
        out_shape=[jax.ShapeDtypeStruct((T, w), BF16) for w in out_w],
        compiler_params=_cparams(),
        name="pre_odd",
    )(x, g, win, gqk, ones, tab3)


def _window_kernel(q_ref, kp_ref, kc_ref, kn_ref, vp_ref, vc_ref, vn_ref, bias_ref, sink_ref, o_ref):
    i = pl.program_id(1)
    nb = pl.num_programs(1)
    q = q_ref[...]
    tw = q.shape[0]
    kw = jnp.concatenate([kp_ref[...], kc_ref[...], kn_ref[...]], axis=0)
    vw = jnp.concatenate([vp_ref[...], vc_ref[...], vn_ref[...]], axis=0)
    col = lax.broadcasted_iota(jnp.int32, (1, tw + 2 * QB), 1)
    outside = ((col < QB) & (i == 0)) | ((col >= QB + tw) & (i == nb - 1))
    lane = lax.broadcasted_iota(jnp.int32, (1, LANES), 1)
    for m in range(4):
        qblk = q[:, 128 * m:128 * (m + 1)]
        outs = []
        for half in range(2):
            h = m + 4 * half
            keep = (lane < HEAD_DIM) if half == 0 else (lane >= HEAD_DIM)
            qm = jnp.where(keep, qblk, jnp.zeros_like(qblk))
            s = lax.dot_general(qm, kw, _NT, preferred_element_type=F32) + bias_ref[h]
            s = jnp.where(outside, NEG_INF, s)
            sk = sink_ref[h:h + 1, 0:1]
            mx = jnp.maximum(jnp.max(s, axis=-1, keepdims=True), sk)
            e = jnp.exp(s - mx)
            den = jnp.sum(e, axis=-1, keepdims=True) + jnp.exp(sk - mx)
            outs.append(jnp.dot(e.astype(BF16), vw, preferred_element_type=F32) / den)
        o_ref[:, 128 * m:128 * (m + 1)] = jnp.where(lane < HEAD_DIM, outs[0], outs[1]).astype(BF16)


def _window_attn(qa, ka, va, bias, sink, B, S, tw):
    T = qa.shape[0]
    nb = S // tw
    r = tw // QB
    nqb = S // QB
    side = lambda d: pl.BlockSpec(
        (QB, 128), lambda b, i: (b * nqb + jnp.clip(r * i + d, 0, nqb - 1), 0))
    mid = pl.BlockSpec((tw, 128), lambda b, i: (b * nb + i, 0))
    return pl.pallas_call(
        _window_kernel,
        grid=(B, nb),
        in_specs=[pl.BlockSpec((tw, 512), lambda b, i: (b * nb + i, 0)),
                  side(-1), mid, side(r), side(-1), mid, side(r),
                  _const_spec(bias.shape), _const_spec(sink.shape)],
        out_specs=pl.BlockSpec((tw, 512), lambda b, i: (b * nb + i, 0)),
        out_shape=jax.ShapeDtypeStruct((T, 512), BF16),
        compiler_params=_cparams(),
        name="window_attn",
    )(qa, ka, ka, ka, va, va, va, bias, sink)


def _pair_kernel(q_ref, k_ref, v_ref, o_ref, vx_ref, *, lanes_a, lanes_b, kc):
    @pl.when(pl.program_id(2) == 0)
    def _():
        vx_ref[:, 0:LANES] = v_ref[...]
        vx_ref[:, LANES:2 * LANES] = jnp.ones((vx_ref.shape[0], LANES), BF16)

    q = q_ref[...]
    tq, wq = q.shape
    S = k_ref.shape[0]
    lane = lax.broadcasted_iota(jnp.int32, (1, wq), 1)
    qms = []
    for ranges in (lanes_a, lanes_b):
        keep = functools.reduce(jnp.logical_or, [(lane >= lo) & (lane < hi) for lo, hi in ranges])
        qms.append(jnp.where(keep, q, jnp.zeros_like(q)))
    m = [jnp.full((tq, 1), NEG_INF, F32) for _ in range(2)]
    acc = [jnp.zeros((tq, 2 * LANES), F32) for _ in range(2)]
    for c in range(S // kc):
        kchunk = k_ref[kc * c:kc * (c + 1), :]
        vchunk = vx_ref[kc * c:kc * (c + 1), :]
        for h in range(2):
            s = lax.dot_general(qms[h], kchunk, _NT, preferred_element_type=F32)
            m_new = jnp.maximum(m[h], jnp.max(s, axis=-1, keepdims=True))
            p = jnp.exp2(s - m_new).astype(BF16)
            acc[h] = jnp.exp2(m[h] - m_new) * acc[h] + jnp.dot(p, vchunk, preferred_element_type=F32)
            m[h] = m_new
    outs = [a[:, 0:LANES] / a[:, LANES:LANES + 1] for a in acc]
    lane_o = lax.broadcasted_iota(jnp.int32, (1, LANES), 1)
    o_ref[...] = jnp.where(lane_o < 64, outs[0], outs[1]).astype(BF16)


def _pair_attn(q, k, v, B, S, tq, wq, shared_kv, lanes_a, lanes_b, name):
    T = q.shape[0]
    nq = S // tq
    kc = _tile(S, 512)
    kv_map = (lambda b, m, i: (b, 0)) if shared_kv else (lambda b, m, i: (b, m))
    return pl.pallas_call(
        functools.partial(_pair_kernel, lanes_a=lanes_a, lanes_b=lanes_b, kc=kc),
        scratch_shapes=[pltpu.VMEM((S, 2 * LANES), BF16)],
        grid=(B, 4, nq),
        in_specs=[pl.BlockSpec((tq, wq), lambda b, m, i: (b * nq + i, m)),
                  pl.BlockSpec((S, wq), kv_map),
                  pl.BlockSpec((S, 128), kv_map)],
        out_specs=pl.BlockSpec((tq, 128), lambda b, m, i: (b * nq + i, m)),
        out_shape=jax.ShapeDtypeStruct((T, 512), BF16),
        compiler_params=_cparams(),
        name=name,
    )(q, k, v)


def _diff_kernel(lam_ref, gsub_ref, q_ref, k_ref, v_ref, bias_ref, o_ref, vx_ref, *, S, tq, kc, lambda_init):
    i = pl.program_id(2)

    @pl.when(i == 0)
    def _():
        vx_ref[:, 0:LANES] = v_ref[...]
        vx_ref[:, LANES:2 * LANES] = jnp.ones((vx_ref.shape[0], LANES), BF16)

    r = tq // QB
    nb = kc // QB
    q = q_ref[...]
    lp = lam_ref[...]
    lam = (jnp.exp(jnp.sum(lp[0:1] * lp[1:2], axis=-1, keepdims=True))
           - jnp.exp(jnp.sum(lp[2:3] * lp[3:4], axis=-1, keepdims=True)) + lambda_init)
    lane = lax.broadcasted_iota(jnp.int32, (1, LANES), 1)
    qms = [jnp.where(lane < HEAD_DIM, q, jnp.zeros_like(q)), jnp.where(lane >= HEAD_DIM, q, jnp.zeros_like(q))]
    m = [jnp.full((tq, 1), NEG_INF, F32) for _ in range(2)]
    acc = [jnp.zeros((tq, 2 * LANES), F32) for _ in range(2)]
    for c in range(S // kc):
        kchunk = k_ref[kc * c:kc * (c + 1), :]
        vchunk = vx_ref[kc * c:kc * (c + 1), :]
        bias = jnp.concatenate(
            [bias_ref[0, jnp.clip(nb * c + j - r * i + 2, 0, r + 3)] for j in range(nb)], axis=1)
        for h in range(2):
            s = lax.dot_general(qms[h], kchunk, _NT, preferred_element_type=F32) + bias
            m_new = jnp.maximum(m[h], jnp.max(s, axis=-1, keepdims=True))
            p = jnp.exp2(s - m_new).astype(BF16)
            acc[h] = jnp.exp2(m[h] - m_new) * acc[h] + jnp.dot(p, vchunk, preferred_element_type=F32)
            m[h] = m_new
    outs = [a[:, 0:LANES] / a[:, LANES:LANES + 1] for a in acc]
    o = outs[0] - lam * outs[1]
    o_ref[...] = (_rms(o, gsub_ref[...]) * (1.0 - lambda_init)).astype(BF16)


def _diff_attn(qd, kd, vd, lam_p, gsub, bias, B, S, tq, lambda_init):
    T = qd.shape[0]
    nq = S // tq
    nt = bias.shape[1]
    kc = _tile(S, 256)
    return pl.pallas_call(
        functools.partial(_diff_kernel, S=S, tq=tq, kc=kc, lambda_init=lambda_init),
        grid=(B, D_HEADS, nq),
        in_specs=[_const_spec(lam_p.shape), _const_spec(gsub.shape),
                  pl.BlockSpec((tq, 128), lambda b, h, i: (b * nq + i, h)),
                  pl.BlockSpec((S, 128), lambda b, h, i: (b, h)),
                  pl.BlockSpec((S, 128), lambda b, h, i: (b, h)),
                  pl.BlockSpec((1, nt, tq, 128), lambda b, h, i: (h, 0, 0, 0))],
        out_specs=pl.BlockSpec((tq, 128), lambda b, h, i: (b * nq + i, h)),
        out_shape=jax.ShapeDtypeStruct((T, 512), BF16),
        scratch_shapes=[pltpu.VMEM((S, 2 * LANES), BF16)],
        compiler_params=_cparams(),
        name="diff_attn",
    )(lam_p, gsub, qd, kd, vd, bias)


def _norm_matmul_kernel(x_ref, g_ref, w_ref, o_ref):
    hn = _rms(x_ref[...], g_ref[...]).astype(BF16)
    o_ref[...] = jnp.dot(hn, w_ref[...], preferred_element_type=F32).astype(o_ref.dtype)


def _norm_matmul(x, g, w, tm, out_dtype):
    T, K = x.shape
    N = w.shape[1]
    return pl.pallas_call(
        _norm_matmul_kernel,
        grid=(T // tm,),
        in_specs=[pl.BlockSpec((tm, K), lambda i: (i, 0)), _const_spec(g.shape), _const_spec(w.shape)],
        out_specs=pl.BlockSpec((tm, N), lambda i: (i, 0)),
        out_shape=jax.ShapeDtypeStruct((T, N), out_dtype),
        compiler_params=_cparams(),
        name="norm_matmul",
    )(x, g, w)


def _post_kernel(x_ref, oa_ref, ob_ref, wout_ref, gx_ref, wq_ref, kv_ref, wo_ref, gf_ref, wr_ref,
                 x2_ref, hn_ref, aff_ref):
    x1 = (x_ref[...]
          + jnp.dot(oa_ref[...], wout_ref[0:512, :], preferred_element_type=F32)
          + jnp.dot(ob_ref[...], wout_ref[512:1024, :], preferred_element_type=F32))
    hc = _rms(x1, gx_ref[...]).astype(BF16)
    q = (jnp.dot(hc, wq_ref[...], preferred_element_type=F32) * (X_HEAD_DIM ** -0.5)).astype(BF16)
    heads = []
    for h in range(X_HEADS):
        lo = X_HEAD_DIM * h
        kh = kv_ref[:, lo:lo + X_HEAD_DIM]
        vh = kv_ref[:, D_MODEL + lo:D_MODEL + lo + X_HEAD_DIM]
        s = lax.dot_general(q[:, lo:lo + X_HEAD_DIM], kh, _NT, preferred_element_type=F32)
        e, l = _softmax_rows(s)
        heads.append((jnp.dot(e.astype(BF16), vh, preferred_element_type=F32) / l).astype(BF16))
    o = jnp.concatenate(heads, axis=1)
    x2 = x1 + jnp.dot(o, wo_ref[...], preferred_element_type=F32)
    x2_ref[...] = x2
    hf = _rms(x2, gf_ref[...])
    hn_ref[...] = hf.astype(BF16)
    tm = hf.shape[0]
    h_hi = hf.astype(BF16)
    h_lo = (hf - h_hi.astype(F32)).astype(BF16)
    r = jnp.dot(jnp.concatenate([h_hi, h_lo], axis=0), wr_ref[...], preferred_element_type=F32)
    top = r[0:tm]
    logits = top + pltpu.roll(top, LANES - N_EXPERTS, 1) + r[tm:2 * tm]
    valid = lax.broadcasted_iota(jnp.int32, (1, LANES), 1) < N_EXPERTS
    mx = jnp.max(jnp.where(valid, logits, NEG_INF), axis=-1, keepdims=True)
    e = jnp.where(valid, jnp.exp(logits - mx), 0.0)
    aff = e / jnp.sum(e, axis=-1, keepdims=True)
    aff_t = aff.T
    for c in range(tm // LANES):
        aff_ref[c] = aff_t[0:N_EXPERTS, LANES * c:LANES * (c + 1)]


def _post(x, oa, ob, wout, gx, wq, kv, wo, gf, wr3, S, n_mem, tm):
    T = x.shape[0]
    per_b = S // tm
    row = lambda w: pl.BlockSpec((tm, w), lambda i: (i, 0))
    return pl.pallas_call(
        _post_kernel,
        grid=(T // tm,),
        in_specs=[row(D_MODEL), row(512), row(512), _const_spec(wout.shape), _const_spec(gx.shape),
                  _const_spec(wq.shape), pl.BlockSpec((n_mem, 2 * D_MODEL), lambda i: (i // per_b, 0)),
                  _const_spec(wo.shape), _const_spec(gf.shape), _const_spec(wr3.shape)],
        out_specs=[row(D_MODEL), row(D_MODEL),
                   pl.BlockSpec((tm // LANES, N_EXPERTS, LANES), lambda i: (i, 0, 0))],
        out_shape=[jax.ShapeDtypeStruct((T, D_MODEL), F32), jax.ShapeDtypeStruct((T, D_MODEL), BF16),
                   jax.ShapeDtypeStruct((T // LANES, N_EXPERTS, LANES), F32)],
        compiler_params=_cparams(),
        name="post",
    )(x, oa, ob, wout, gx, wq, kv, wo, gf, wr3)


def _route_kernel(aff_ref, tri_ref, ones_ref, carrymat_ref, key_ref, carry_ref, *, cap):
    x = aff_ref[...]
    nj = x.shape[0]

    def count(mask):
        c = jnp.sum(mask.astype(F32), axis=0, keepdims=True)
        return jnp.sum(c, axis=2, keepdims=True)

    def bisect(b, thr):
        cand = thr | lax.shift_left(jnp.int32(1), 30 - b)
        return jnp.where(count(x >= pltpu.bitcast(cand, F32)) >= cap, cand, thr)

    thr = pltpu.bitcast(lax.fori_loop(0, 31, bisect, jnp.zeros((1, N_EXPERTS, 1), jnp.int32)), F32)
    gt = x > thr
    eq = x == thr
    need = cap - count(gt)

    def prefix(mask):
        v = mask.astype(BF16).reshape(nj * N_EXPERTS, LANES)
        within = jnp.dot(v, tri_ref[...], preferred_element_type=F32)
        tot = jnp.dot(v, ones_ref[...], preferred_element_type=F32)
        carry = jnp.dot(carrymat_ref[...], tot.astype(BF16), preferred_element_type=F32)
        return within + carry, carry

    eq_rank, _ = prefix(eq)
    sel = gt | (eq & (eq_rank.reshape(nj, N_EXPERTS, LANES) < need))
    rank, carry = prefix(sel)
    key_ref[...] = jnp.where(sel, rank.reshape(nj, N_EXPERTS, LANES), -1.0)
    carry_ref[...] = carry


def _route(aff3, tri, ones, carrymat, cap):
    nj = aff3.shape[0]
    rows = nj * N_EXPERTS
    return pl.pallas_call(
        functools.partial(_route_kernel, cap=cap),
        out_shape=[jax.ShapeDtypeStruct((nj, N_EXPERTS, LANES), F32),
                   jax.ShapeDtypeStruct((rows, LANES), F32)],
        compiler_params=_cparams(),
        name="route",
    )(aff3, tri, ones, carrymat)


def _window_start(c0, cap, win):
    a = jnp.minimum((c0 // 16) * 16, cap - win)
    return pl.multiple_of(a, 16)


GATHER_EXPERTS = 2


def _gather_kernel(c0_ref, key_ref, aff_ref, hn_ref, xe_ref, gs_ref, *, cap, win, nq):
    sb = pl.program_id(1)

    @pl.when(sb == 0)
    def _():
        xe_ref[...] = jnp.zeros_like(xe_ref)
        gs_ref[...] = jnp.zeros_like(gs_ref)

    slot = lax.broadcasted_iota(jnp.int32, (win, 1), 0).astype(F32)

    def lane_row(ref, j, q):
        return jnp.concatenate([ref[j, 2 * q:2 * q + 1, :], ref[j, 2 * q + 1:2 * q + 2, :]], axis=1)

    def add_window(j, q, a, lo):
        keyrow = lane_row(key_ref, j, q)
        hit = (keyrow - a.astype(F32) == slot) & (keyrow >= lo.astype(F32))
        rows = jnp.dot(hit.astype(BF16), hn_ref[256 * q:256 * (q + 1), :],
                       preferred_element_type=F32)
        xe_ref[j, pl.ds(a, win), :] = xe_ref[j, pl.ds(a, win), :] + rows.astype(BF16)
        gates = jnp.sum(jnp.where(hit, lane_row(aff_ref, j, q), 0.0), axis=-1, keepdims=True)
        gs_ref[j, pl.ds(a, win), :] = gs_ref[j, pl.ds(a, win), :] + gates

    experts = [pl.program_id(0) * GATHER_EXPERTS + j for j in range(GATHER_EXPERTS)]
    starts = [[_window_start(c0_ref[e, sb * nq + q], cap, win) for q in range(nq)] for e in experts]
    for q in range(nq):
        for j, e in enumerate(experts):
            add_window(j, q, starts[j][q], c0_ref[e, sb * nq + q])
    extra_windows = -(-(256 + 15) // win) - 1
    ends = [[c0_ref[e, sb * nq + q + 1] for q in range(nq)] for e in experts]
    overflow = functools.reduce(jnp.logical_or, [ends[j][q] > starts[j][q] + win
                                                 for j in range(GATHER_EXPERTS) for q in range(nq)])

    @pl.when(overflow)
    def _():
        for q in range(nq):
            for j in range(GATHER_EXPERTS):
                for w in range(1, extra_windows + 1):
                    nxt = starts[j][q] + w * win
                    pl.when(ends[j][q] > nxt)(
                        functools.partial(add_window, j, q, _window_start(nxt, cap, win), nxt))


def _gather(c0, key_em, aff_em, hn, cap, win, tsb):
    T, D = hn.shape
    nq = tsb // 256
    ge = GATHER_EXPERTS
    idx_spec = pl.BlockSpec((ge, tsb // LANES, LANES), lambda e, sb, c0: (e, sb, 0))
    grid_spec = pltpu.PrefetchScalarGridSpec(
        num_scalar_prefetch=1,
        grid=(N_EXPERTS // ge, T // tsb),
        in_specs=[idx_spec, idx_spec, pl.BlockSpec((tsb, D), lambda e, sb, c0: (sb, 0))],
        out_specs=[pl.BlockSpec((ge, cap, D), lambda e, sb, c0: (e, 0, 0)),
                   pl.BlockSpec((ge, cap, LANES), lambda e, sb, c0: (e, 0, 0))],
    )
    return pl.pallas_call(
        functools.partial(_gather_kernel, cap=cap, win=win, nq=nq),
        grid_spec=grid_spec,
        out_shape=[jax.ShapeDtypeStruct((N_EXPERTS, cap, D), BF16),
                   jax.ShapeDtypeStruct((N_EXPERTS, cap, LANES), F32)],
        compiler_params=_cparams(),
        name="moe_gather",
    )(c0, key_em, aff_em, hn)


def _ffn_kernel(xe_ref, gs_ref, wg_ref, wu_ref, wd_ref, y_ref, acc_ref):
    f = pl.program_id(2)

    @pl.when(f == 0)
    def _():
        acc_ref[...] = jnp.zeros_like(acc_ref)

    xe = xe_ref[0]
    a = jnp.dot(xe, wg_ref[0, 0].astype(BF16), preferred_element_type=F32)
    u = jnp.dot(xe, wu_ref[0, 0].astype(BF16), preferred_element_type=F32)
    h = (a * (1.0 / (1.0 + jnp.exp(-a))) * u).astype(BF16)
    acc_ref[...] += jnp.dot(h, wd_ref[0, 0].astype(BF16), preferred_element_type=F32)

    @pl.when(f == pl.num_programs(2) - 1)
    def _():
        y = (acc_ref[...] * gs_ref[0][:, 0:1]).astype(BF16)
        dh = y_ref.shape[3]
        for h in range(y_ref.shape[1]):
            y_ref[0, h] = y[:, dh * h:dh * (h + 1)]


def _expert_ffn(xe, gs, wg, wu, wd, layer, tc, tf, dh):
    E, cap, D = xe.shape
    F = wg.shape[3]
    return pl.pallas_call(
        _ffn_kernel,
        grid=(E, cap // tc, F // tf),
        in_specs=[pl.BlockSpec((1, tc, D), lambda e, c, f: (e, c, 0)),
                  pl.BlockSpec((1, tc, LANES), lambda e, c, f: (e, c, 0)),
                  pl.BlockSpec((1, 1, D, tf), lambda e, c, f: (layer, e, 0, f)),
                  pl.BlockSpec((1, 1, D, tf), lambda e, c, f: (layer, e, 0, f)),
                  pl.BlockSpec((1, 1, tf, D), lambda e, c, f: (layer, e, f, 0))],
        out_specs=pl.BlockSpec((1, D // dh, tc, dh), lambda e, c, f: (e, 0, c, 0)),
        out_shape=jax.ShapeDtypeStruct((E, D // dh, cap, dh), BF16),
        scratch_shapes=[pltpu.VMEM((tc, D), F32)],
        compiler_params=_cparams(),
        name="expert_ffn",
    )(xe, gs, wg, wu, wd)


COMBINE_EXPERTS = 2


def _combine_kernel(c0_ref, x_ref, key_ref, y_ref, o_ref, *, cap, win, nq):
    sb = pl.program_id(1)

    @pl.when(pl.program_id(2) == 0)
    def _():
        o_ref[...] = x_ref[...]

    experts = [pl.program_id(2) * COMBINE_EXPERTS + j for j in range(COMBINE_EXPERTS)]
    lane = lax.broadcasted_iota(jnp.int32, (1, N_EXPERTS), 1)
    keycols = [jnp.sum(jnp.where(lane == e, key_ref[...], 0.0), axis=-1, keepdims=True) for e in experts]
    slot = lax.broadcasted_iota(jnp.int32, (1, win), 1).astype(F32)

    def one_hot(j, q, a, lo):
        kq = keycols[j][256 * q:256 * (q + 1)]
        hit = kq - a.astype(F32) == slot
        if lo is not None:
            hit = hit & (kq >= lo.astype(F32))
        return hit.astype(BF16)

    starts = [[_window_start(c0_ref[e, sb * nq + q], cap, win) for q in range(nq)] for e in experts]
    for q in range(nq):
        hits = jnp.concatenate([one_hot(j, q, starts[j][q], None) for j in range(COMBINE_EXPERTS)], axis=1)
        rows = jnp.concatenate([y_ref[j, 0, pl.ds(starts[j][q], win), :] for j in range(COMBINE_EXPERTS)], axis=0)
        o_ref[256 * q:256 * (q + 1), :] = (o_ref[256 * q:256 * (q + 1), :]
                                           + jnp.dot(hits, rows, preferred_element_type=F32))
    extra_windows = -(-(256 + 15) // win) - 1
    ends = [[c0_ref[e, sb * nq + q + 1] for q in range(nq)] for e in experts]
    overflow = functools.reduce(jnp.logical_or, [ends[j][q] > starts[j][q] + win
                                                 for j in range(COMBINE_EXPERTS) for q in range(nq)])

    @pl.when(overflow)
    def _():
        for q in range(nq):
            for j in range(COMBINE_EXPERTS):
                for w in range(1, extra_windows + 1):
                    nxt = starts[j][q] + w * win

                    @pl.when(ends[j][q] > nxt)
                    def _(j=j, q=q, nxt=nxt):
                        a = _window_start(nxt, cap, win)
                        o_ref[256 * q:256 * (q + 1), :] = (
                            o_ref[256 * q:256 * (q + 1), :]
                            + jnp.dot(one_hot(j, q, a, nxt), y_ref[j, 0, pl.ds(a, win), :],
                                      preferred_element_type=F32))


def _combine(c0, x, key_tm, y, cap, win, tsb, dh):
    T, D = x.shape
    nq = tsb // 256
    ce = COMBINE_EXPERTS
    grid_spec = pltpu.PrefetchScalarGridSpec(
        num_scalar_prefetch=1,
        grid=(D // dh, T // tsb, N_EXPERTS // ce),
        in_specs=[pl.BlockSpec((tsb, dh), lambda h, sb, e, c0: (sb, h), pipeline_mode=pl.Buffered(1)),
                  pl.BlockSpec((tsb, N_EXPERTS), lambda h, sb, e, c0: (sb, 0)),
                  pl.BlockSpec((ce, 1, cap, dh), lambda h, sb, e, c0: (e, h, 0, 0))],
        out_specs=pl.BlockSpec((tsb, dh), lambda h, sb, e, c0: (sb, h)),
    )
    return pl.pallas_call(
        functools.partial(_combine_kernel, cap=cap, win=win, nq=nq),
        grid_spec=grid_spec,
        out_shape=jax.ShapeDtypeStruct((T, D), F32),
        compiler_params=_cparams(),
        name="moe_combine",
    )(c0, x, key_tm, y)


def _route_constants(nj):
    lane = jnp.arange(LANES)
    tri = (lane[:, None] < lane[None, :]).astype(BF16)
    ones = jnp.ones((LANES, LANES), BF16)
    r = jnp.arange(nj * N_EXPERTS)
    carrymat = ((r[None, :] % N_EXPERTS == r[:, None] % N_EXPERTS) & (r[None, :] < r[:, None])).astype(BF16)
    return tri, ones, carrymat


def _moe(x2, hn, aff3, wg, wu, wd, layer, cap, tc, tf):
    T, D = x2.shape
    nj = T // LANES
    win = min(128, cap)
    tsb = min(2048, T)
    key3, carry = _route(aff3, *_route_constants(nj), cap)
    key_em = key3.transpose(1, 0, 2)
    aff_em = aff3.transpose(1, 0, 2)
    key_tm = key3.transpose(0, 2, 1).reshape(T, N_EXPERTS)
    starts = carry[:, 0].reshape(nj, N_EXPERTS)[::2].T.astype(jnp.int32)
    c0 = jnp.concatenate([starts, jnp.full((N_EXPERTS, 1), cap, jnp.int32)], axis=1)
    xe, gs = _gather(c0, key_em, aff_em, hn, cap, win, tsb)
    dh = D // 2
    y = _expert_ffn(xe, gs, wg, wu, wd, layer, tc, tf, dh)
    return _combine(c0, x2, key_tm, y, cap, win, min(4096, T), dh)


def _final_norm_kernel(x_ref, g_ref, o_ref):
    o_ref[...] = _rms(x_ref[...], g_ref[...])


def _final_norm(x, g, tm):
    T, D = x.shape
    return pl.pallas_call(
        _final_norm_kernel,
        grid=(T // tm,),
        in_specs=[pl.BlockSpec((tm, D), lambda i: (i, 0)), _const_spec(g.shape)],
        out_specs=pl.BlockSpec((tm, D), lambda i: (i, 0)),
        out_shape=jax.ShapeDtypeStruct((T, D), F32),
        name="final_norm",
    )(x, g)


def _t5_bucket(rel):
    half = N_BUCKETS // 2
    exact = half // 2
    n = jnp.abs(rel)
    big = exact + (jnp.log(jnp.maximum(n, 1).astype(F32) / exact)
                   / math.log(MAX_DISTANCE / exact) * (half - exact)).astype(jnp.int32)
    big = jnp.minimum(big, half - 1)
    return jnp.where(rel > 0, half, 0) + jnp.where(n < exact, n, big)


def _rope_freqs(pos, dim):
    freqs = ROPE_BASE ** (-jnp.arange(0, dim, 2, dtype=F32) / dim)
    ang = pos.astype(F32)[:, None] * freqs[None, :]
    return jnp.cos(ang), jnp.sin(ang)


def _rope_lane_table(cos, sin, active):
    w = cos.shape[1]
    first = (jnp.arange(w) % 32) < 16
    c = jnp.where(active[None, :], cos, 1.0)
    s1 = jnp.where((active & first)[None, :], -sin, 0.0)
    s2 = jnp.where((active & ~first)[None, :], sin, 0.0)
    return jnp.stack([c, s1, s2]).astype(F32)


def _pair_heads_cols(w):
    k = w.shape[0]
    return w.reshape(k, 2, 4, HEAD_DIM).transpose(0, 2, 1, 3).reshape(k, 8 * HEAD_DIM)


def _pair_heads_rows(w):
    n = w.shape[1]
    return w.reshape(2, 4, HEAD_DIM, n).transpose(1, 0, 2, 3).reshape(8 * HEAD_DIM, n)


def _bias_lookup(table, bucket):
    shape = (table.shape[1],) + (1,) * bucket.ndim
    out = jnp.zeros((table.shape[1],) + bucket.shape, F32)
    for b in range(N_BUCKETS):
        out = jnp.where(bucket[None] == b, table[b].reshape(shape), out)
    return out


def _tables(S, tq_diff, tw, rel_bias):
    t = jnp.arange(S)
    lane128 = jnp.arange(128)
    lane256 = jnp.arange(256)
    cos_t, sin_t = _rope_freqs(t, B_ROPE)
    scale_b = LOG2E * (B_NOPE + B_ROPE) ** -0.5
    tq = _rope_lane_table(jnp.tile(cos_t, (1, 16)), jnp.tile(sin_t, (1, 16)),
                          (lane256 >= 128) & (lane256 < 192)) * scale_b
    tk = _rope_lane_table(jnp.tile(cos_t, (1, 8)), jnp.tile(sin_t, (1, 8)), lane128 < 64)
    cos_r, sin_r = _rope_freqs(t // GRID_W, HEAD_DIM // 2)
    cos_c, sin_c = _rope_freqs(t % GRID_W, HEAD_DIM // 2)
    cos_ax = jnp.tile(jnp.concatenate([cos_r, cos_r, cos_c, cos_c], axis=1), (1, 2))
    sin_ax = jnp.tile(jnp.concatenate([sin_r, sin_r, sin_c, sin_c], axis=1), (1, 2))
    tax = _rope_lane_table(cos_ax, sin_ax, jnp.ones((128,), bool))
    table = rel_bias.astype(F32)
    rel_win = jnp.arange(tw + 2 * QB)[None, :] - QB - jnp.arange(tw)[:, None]
    a_bias = _bias_lookup(table[:, :A_HEADS], _t5_bucket(rel_win))
    a_bias = jnp.where((jnp.abs(rel_win) <= WINDOW)[None], a_bias, NEG_INF)
    r = tq_diff // QB
    rel_d = (QB * (jnp.arange(r + 4)[:, None, None] - 2)
             + jnp.arange(QB)[None, None, :] - jnp.arange(tq_diff)[None, :, None])
    d_bias = _bias_lookup(table[:, A_HEADS:], _t5_bucket(rel_d)) * LOG2E
    return dict(tq=tq, tk=tk, tax=tax, a_bias=a_bias, d_bias=d_bias)


def _prep_weights(p):
    row = lambda v: v.reshape(1, -1).astype(F32)
    L = {}
    L['norm_mix'] = [row(p['norm_mix'][l]) for l in range(DEPTH)]
    L['norm_cross'] = [row(p['norm_cross'][l]) for l in range(DEPTH)]
    L['norm_mem'] = [row(p['norm_mem'][l]) for l in range(DEPTH)]
    L['norm_ffn'] = [row(p['norm_ffn'][l]) for l in range(DEPTH)]
    even, odd = [], []
    for i in range((DEPTH + 1) // 2):
        w = p['w_in_even'][i]
        kr = w[:, 1408:1440]
        win = jnp.concatenate([_pair_heads_cols(w[:, 0:512]), w[:, 512:1408], kr, kr,
                               jnp.zeros((D_MODEL, 64), w.dtype)], axis=1).astype(BF16)
        uq = p['b_w_uq'][i].reshape(B_Q_LORA, B_HEADS, B_NOPE + B_ROPE)
        blocks = []
        for m in range(4):
            blocks += [uq[:, 2 * m, :B_NOPE], uq[:, 2 * m + 1, :B_NOPE],
                       uq[:, 2 * m, B_NOPE:], uq[:, 2 * m + 1, B_NOPE:],
                       jnp.zeros((B_Q_LORA, 64), uq.dtype)]
        wuq = jnp.concatenate(blocks, axis=1).astype(BF16)
        ukv = p['b_w_ukv'][i].reshape(B_KV_LORA, B_HEADS, B_NOPE + B_V)
        wukv = jnp.concatenate([ukv[:, :, :B_NOPE].reshape(B_KV_LORA, -1),
                                ukv[:, :, B_NOPE:].reshape(B_KV_LORA, -1)], axis=1).astype(BF16)
        wo = p['w_out_even'][i]
        wout = jnp.concatenate([_pair_heads_rows(wo[0:512]), wo[512:1024]], axis=0).astype(BF16)
        sink = jnp.broadcast_to(p['a_sink'][i].astype(F32)[:, None], (A_HEADS, LANES))
        even.append(dict(win=win, wuq=wuq, wukv=wukv, wout=wout, sink=sink,
                         gq=row(p['b_q_norm'][i]), gkv=row(p['b_kv_norm'][i])))
    for i in range(DEPTH // 2):
        w = p['w_in_odd'][i]
        win = jnp.concatenate([_pair_heads_cols(w[:, 0:512]), w[:, 512:]], axis=1).astype(BF16)
        gqk = jnp.concatenate([jnp.tile(p['c_q_norm'][i], C_HEADS),
                               jnp.tile(p['c_k_norm'][i], C_KV_HEADS)]).reshape(1, -1).astype(F32)
        wo = p['w_out_odd'][i]
        wout = jnp.concatenate([_pair_heads_rows(wo[0:512]), wo[512:1024]], axis=0).astype(BF16)
        odd.append(dict(win=win, gqk=gqk, wout=wout, lam=p['d_lambda'][i].astype(F32),
                        gsub=row(p['d_subln'][i])))
    L['even'], L['odd'] = even, odd
    seg = jnp.arange(640) // HEAD_DIM
    L['ones'] = (seg[:, None] == seg[None, :]).astype(BF16)
    L['x_wq'] = [p['x_wq'][l].astype(BF16) for l in range(DEPTH)]
    L['x_wkv'] = [p['x_wkv'][l].astype(BF16) for l in range(DEPTH)]
    L['x_wo'] = [p['x_wo'][l].astype(BF16) for l in range(DEPTH)]
    wr = []
    for l in range(DEPTH):
        w = p['router'][l].astype(F32)
        w_hi = w.astype(BF16)
        w_lo = (w - w_hi.astype(F32)).astype(BF16)
        wr.append(jnp.concatenate([w_hi, w_lo, jnp.zeros((D_MODEL, LANES - 2 * N_EXPERTS), BF16)], axis=1))
    L['router'] = wr
    L['norm_final'] = row(p['norm_final'])
    return L


def _tile(n, pref):
    t = min(n, pref)
    assert n % t == 0, (n, t)
    return t


def _trunk(x, mem, p, L):
    B, S, D = x.shape
    n_mem = mem.shape[1]
    T = B * S
    tm_pre = _tile(S, 1024)
    tm_post = _tile(S, 512)
    tq = _tile(S, 512)
    tq_pair = _tile(S, 1024)
    cap = EC_CAPACITY * T // N_EXPERTS
    tc = _tile(cap, 1024)
    tf = 512
    tw = _tile(S, 256)
    tabs = _tables(S, tq, tw, p['rel_bias'])
    xf = x.reshape(T, D).astype(F32)
    memf = mem.reshape(B * n_mem, D).astype(F32)
    tm_mem = _tile(B * n_mem, 256)
    mla_a = ((0, 64), (128, 160))
    mla_b = ((64, 128), (160, 192))
    for layer in range(DEPTH):
        i = layer // 2
        if layer % 2 == 0:
            w = L['even'][i]
            qa, ka, va, qcat, kcat, vb = _pre_even(xf, S, L['norm_mix'][layer], w['win'], w['gq'], w['gkv'],
                                                   w['wuq'], w['wukv'], tabs['tq'], tabs['tk'], tm_pre)
            o1 = _window_attn(qa, ka, va, tabs['a_bias'], w['sink'], B, S, tw)
            o2 = _pair_attn(qcat, kcat, vb, B, S, tq_pair, 256, False, mla_a, mla_b, "latent_attn")
        else:
            w = L['odd'][i]
            qc, kc, vc, qd, kd, vd = _pre_odd(xf, S, L['norm_mix'][layer], w['win'], w['gqk'], L['ones'],
                                              tabs['tax'], tm_pre)
            o1 = _pair_attn(qc, kc, vc, B, S, tq_pair, 128, True, ((0, 64),), ((64, 128),), "axial_attn")
            lambda_init = 0.8 - 0.6 * math.exp(-0.3 * layer)
            o2 = _diff_attn(qd, kd, vd, w['lam'], w['gsub'], tabs['d_bias'], B, S, tq, lambda_init)
        kv = _norm_matmul(memf, L['norm_mem'][layer], L['x_wkv'][layer], tm_mem, BF16)
        x2, hn, aff = _post(xf, o1, o2, w['wout'], L['norm_cross'][layer], L['x_wq'][layer], kv,
                            L['x_wo'][layer], L['norm_ffn'][layer], L['router'][layer], S, n_mem, tm_post)
        xf = _moe(x2, hn, aff, p['e_w_gate'], p['e_w_up'], p['e_w_down'], layer, cap, tc, tf)
    out = _final_norm(xf, L['norm_final'], tm_pre)
    return out.reshape(B, S, D)


def kernel(x_prompt, x_sample, mem_prompt, mem_sample, rel_bias, norm_mix, norm_cross, norm_mem, norm_ffn,
           norm_final, w_in_even, a_sink, b_q_norm, b_kv_norm, b_w_uq, b_w_ukv, w_out_even, w_in_odd,
           c_q_norm, c_k_norm, d_lambda, d_subln, w_out_odd, x_wq, x_wkv, x_wo, router, e_w_gate,
           e_w_up, e_w_down):
    p = dict(rel_bias=rel_bias, norm_mix=norm_mix, norm_cross=norm_cross, norm_mem=norm_mem,
             norm_ffn=norm_ffn, norm_final=norm_final, w_in_even=w_in_even, a_sink=a_sink,
             b_q_norm=b_q_norm, b_kv_norm=b_kv_norm, b_w_uq=b_w_uq, b_w_ukv=b_w_ukv,
             w_out_even=w_out_even, w_in_odd=w_in_odd, c_q_norm=c_q_norm, c_k_norm=c_k_norm,
             d_lambda=d_lambda, d_subln=d_subln, w_out_odd=w_out_odd, x_wq=x_wq, x_wkv=x_wkv,
             x_wo=x_wo, router=router, e_w_gate=e_w_gate, e_w_up=e_w_up, e_w_down=e_w_down)
    L = _prep_weights(p)
    y_prompt = _trunk(x_prompt, mem_prompt, p, L)
    y_sample = _trunk(x_sample, mem_sample, p, L)
    return (y_prompt, y_sample)
```
